```python
import jax, jax.numpy as jnp
from jax import lax
import numpy as np


D_MODEL = 1024
BATCH = 16
SEQ = 2048
DEPTH = 1
DEC_BATCH = 128
DEC_SEQ = 4
PAST_LEN = 8192
PAGE_SIZE = 128

N_META = 16
MIX_WIDTH = D_MODEL
GLA_HEADS = 4
GLA_DV = MIX_WIDTH // 2 // GLA_HEADS
GLA_DK = GLA_DV // 2
GLA_GATE_RANK = 16
GLA_TAU = 16.0
GLA_CHUNK = 64
MLA_HEADS = 8
MLA_DV = MIX_WIDTH // 2 // MLA_HEADS
MLA_NOPE = 64
MLA_ROPE = 32
MLA_Q_LORA = 384
MLA_KV_LORA = 256
MLA_SCALE = (MLA_NOPE + MLA_ROPE) ** -0.5
ROPE_BASE = 10000.0
Q_BLOCK = 128
D_FF = 2816
CONV_W = 3
DN_ALPHA = (2.0 * DEPTH) ** 0.25
DN_BETA = (8.0 * DEPTH) ** -0.25
NORM_EPS = 1e-5
NEG_INF = -1e30

IN_SIZES = (GLA_HEADS * GLA_DK, GLA_HEADS * GLA_DK, GLA_HEADS * GLA_DV, GLA_HEADS * GLA_DV,
            GLA_GATE_RANK, MLA_Q_LORA, MLA_KV_LORA, MLA_ROPE)
IN_OFFSETS = tuple(int(o) for o in np.cumsum(IN_SIZES)[:-1])
IN_COLS = int(sum(IN_SIZES))
V_START = 2 * GLA_HEADS * GLA_DK
V_END = V_START + GLA_HEADS * GLA_DV

kernel_name = 'hymba_gla_mla_deepnorm_step'


def layer_norm(x, g, b):
    xf = x.astype(jnp.float32)
    mu = jnp.mean(xf, axis=-1, keepdims=True)
    var = jnp.mean(jnp.square(xf - mu), axis=-1, keepdims=True)
    return ((xf - mu) * lax.rsqrt(var + NORM_EPS) * g + b).astype(x.dtype)


def rms_norm(x, g):
    xf = x.astype(jnp.float32)
    ms = jnp.mean(jnp.square(xf), axis=-1, keepdims=True)
    return (xf * lax.rsqrt(ms + NORM_EPS) * g).astype(x.dtype)


def rope(x, pos):
    half = MLA_ROPE // 2
    inv = ROPE_BASE ** (-jnp.arange(half, dtype=jnp.float32) / half)
    ang = pos.astype(jnp.float32)[:, None] * inv[None, :]
    cos = jnp.cos(ang)[:, None, :]
    sin = jnp.sin(ang)[:, None, :]
    xf = x.astype(jnp.float32)
    x1, x2 = xf[..., :half], xf[..., half:]
    return jnp.concatenate([x1 * cos - x2 * sin, x2 * cos + x1 * sin], axis=-1).astype(x.dtype)


def mixer_inputs(h, pos, w_in, w_gate_up, b_gate, mla_q_norm_g, mla_kv_norm_g, w_uq, w_uk):
    B, T, _ = h.shape
    z = h @ w_in
    q_g, k_g, v_g, g_g, a_lr, c_q, c_kv, k_r = jnp.split(z, IN_OFFSETS, axis=-1)
    q_g = q_g.reshape(B, T, GLA_HEADS, GLA_DK) * (GLA_DK ** -0.5)
    k_g = k_g.reshape(B, T, GLA_HEADS, GLA_DK)
    v_g = v_g.reshape(B, T, GLA_HEADS, GLA_DV)
    log_a = jax.nn.log_sigmoid((a_lr @ w_gate_up + b_gate).astype(jnp.float32)) / GLA_TAU
    log_a = log_a.reshape(B, T, GLA_HEADS, GLA_DK)
    q = (rms_norm(c_q, mla_q_norm_g) @ w_uq).reshape(B, T, MLA_HEADS, MLA_NOPE + MLA_ROPE)
    q_nope, q_rope = q[..., :MLA_NOPE], rope(q[..., MLA_NOPE:], pos)
    q_lat = jnp.einsum('bthn,chn->bthc', q_nope, w_uk.reshape(MLA_KV_LORA, MLA_HEADS, MLA_NOPE))
    ckv = rms_norm(c_kv, mla_kv_norm_g)
    krope = rope(k_r[:, :, None, :], pos)[:, :, 0, :]
    return (q_g, k_g, v_g, log_a, g_g), (q_lat, q_rope, ckv, krope)


def gla_chunked(q, k, v, log_a, s0, chunk):
    B, T, H, _ = q.shape
    DV = v.shape[-1]
    n = T // chunk
    f32 = jnp.float32

    def blocks(a):
        return a.astype(f32).reshape(B, n, chunk, H, a.shape[-1]).transpose(1, 0, 3, 2, 4)

    qc, kc, vc, gc = blocks(q), blocks(k), blocks(v), blocks(log_a)
    causal = jnp.tril(jnp.ones((chunk, chunk), dtype=bool))
    mid = chunk // 2

    def step(s, inp):
        qi, ki, vi, gi = inp
        b = jnp.cumsum(gi, axis=2)
        b_last = b[:, :, -1:, :]
        b_mid = b[:, :, mid:mid + 1, :]
        o_inter = jnp.einsum('bhcd,bhde->bhce', qi * jnp.exp(b), s)
        att = jnp.einsum('bhid,bhjd->bhij', qi * jnp.exp(b - b_mid), ki * jnp.exp(b_mid - b))
        att = jnp.where(causal, att, 0.0)
        o_intra = jnp.einsum('bhij,bhje->bhie', att, vi)
        s_new = s * jnp.exp(b_last)[:, :, 0, :, None] + jnp.einsum(
            'bhcd,bhce->bhde', ki * jnp.exp(b_last - b), vi)
        return s_new, o_inter + o_intra

    s_fin, o = lax.scan(step, s0.astype(f32), (qc, kc, vc, gc))
    o = o.transpose(1, 0, 3, 2, 4).reshape(B, T, H, DV)
    return o, s_fin


def mla_prompt(q_lat, q_rope, ckv, krope):
    B, L = q_lat.shape[:2]
    nb = -(-L // Q_BLOCK)
    pad = nb * Q_BLOCK - L

    def qblocks(a):
        a = jnp.pad(a, ((0, 0), (0, pad), (0, 0), (0, 0)))
        return a.reshape(B, nb, Q_BLOCK, *a.shape[2:]).transpose(1, 0, 2, 3, 4)

    qpos = jnp.arange(nb * Q_BLOCK, dtype=jnp.int32).reshape(nb, Q_BLOCK)
    kpos = jnp.arange(L, dtype=jnp.int32)

    def blk(args):
        ql, qr, pq = args
        s = (jnp.einsum('bthc,bsc->bhts', ql, ckv) +
             jnp.einsum('bthr,bsr->bhts', qr, krope)).astype(jnp.float32) * MLA_SCALE
        s = jnp.where(kpos[None, None, None, :] <= pq[None, None, :, None], s, NEG_INF)
        p = jax.nn.softmax(s, axis=-1).astype(ckv.dtype)
        return jnp.einsum('bhts,bsc->bthc', p, ckv)

    o = lax.map(blk, (qblocks(q_lat), qblocks(q_rope), qpos))
    return o.transpose(1, 0, 2, 3, 4).reshape(B, nb * Q_BLOCK, MLA_HEADS, MLA_KV_LORA)[:, :L]


def mla_sample(q_lat, q_rope, ckv_new, krope_new, cache_ckv, cache_krope, page_table):
    Bd, T = q_lat.shape[:2]
    past_ckv = cache_ckv[page_table].reshape(Bd, -1, MLA_KV_LORA)
    past_kr = cache_krope[page_table].reshape(Bd, -1, MLA_ROPE)
    P = past_ckv.shape[1]
    s_past = jnp.einsum('bthc,bsc->bhts', q_lat, past_ckv) + jnp.einsum('bthr,bsr->bhts', q_rope, past_kr)
    s_new = jnp.einsum('bthc,bsc->bhts', q_lat, ckv_new) + jnp.einsum('bthr,bsr->bhts', q_rope, krope_new)
    causal = jnp.tril(jnp.ones((T, T), dtype=bool))
    s_new = jnp.where(causal, s_new.astype(jnp.float32) * MLA_SCALE, NEG_INF)
    s = jnp.concatenate([s_past.astype(jnp.float32) * MLA_SCALE, s_new], axis=-1)
    p = jax.nn.softmax(s, axis=-1).astype(ckv_new.dtype)
    return (jnp.einsum('bhts,bsc->bthc', p[..., :P], past_ckv) +
            jnp.einsum('bhts,bsc->bthc', p[..., P:], ckv_new))


def mixer_output(o_gla, g_g, o_lat, gla_norm_g, w_uv, w_o):
    B, T = o_gla.shape[:2]
    gate = jax.nn.silu(g_g.reshape(B, T, GLA_HEADS, GLA_DV))
    og = (rms_norm(o_gla, gla_norm_g) * gate).reshape(B, T, GLA_HEADS * GLA_DV)
    om = jnp.einsum('bthc,chv->bthv', o_lat, w_uv.reshape(MLA_KV_LORA, MLA_HEADS, MLA_DV))
    om = om.reshape(B, T, MLA_HEADS * MLA_DV)
    return jnp.concatenate([og, om], axis=-1) @ w_o


def conv_ffn(h, conv_prev, w_ffn_in, conv_w, conv_b, w_down):
    T = h.shape[1]
    a, up = jnp.split(h @ w_ffn_in, [D_FF], axis=-1)
    ext = jnp.concatenate([conv_prev.astype(a.dtype), a], axis=1)
    conv = conv_b + sum(conv_w[j] * ext[:, j:j + T] for j in range(CONV_W))
    return (jax.nn.silu(conv) * up) @ w_down, ext[:, T:]


def setup_inputs(seed: int = 0) -> dict:
    key = jax.random.key(seed)
    ks = jax.random.split(key, 32)
    n_pages = PAST_LEN // PAGE_SIZE
    n_used = DEC_BATCH * n_pages
    n_pool = n_used + n_used // 4
    f32 = jnp.float32
    nrm = lambda k, shape, s=1.0: jax.random.normal(k, shape, f32) * s
    page_table = jax.random.permutation(ks[0], n_pool)[:n_used].reshape(DEC_BATCH, n_pages).astype(jnp.int32)
    w_in = nrm(ks[8], (D_MODEL, IN_COLS), D_MODEL ** -0.5)
    w_in = w_in.at[:, V_START:V_END].multiply(DN_BETA)
    return {
        'x_prompt': nrm(ks[1], (BATCH, SEQ, D_MODEL)),
        'x_sample': nrm(ks[2], (DEC_BATCH, DEC_SEQ, D_MODEL)),
        'cache_ckv': nrm(ks[3], (n_pool, PAGE_SIZE, MLA_KV_LORA)),
        'cache_krope': nrm(ks[4], (n_pool, PAGE_SIZE, MLA_ROPE)),
        'page_table': page_table,
        'state_gla': nrm(ks[5], (DEC_BATCH, GLA_HEADS, GLA_DK, GLA_DV), 0.3),
        'state_conv': nrm(ks[6], (DEC_BATCH, CONV_W - 1, D_FF)),
        'meta_tokens': nrm(ks[7], (N_META, D_MODEL)),
        'ln_emb_g': 1.0 + nrm(ks[9], (D_MODEL,), 0.02),
        'ln_emb_b': nrm(ks[10], (D_MODEL,), 0.02),
        'w_in': w_in,
        'w_gate_up': nrm(ks[11], (GLA_GATE_RANK, GLA_HEADS * GLA_DK), GLA_GATE_RANK ** -0.5),
        'b_gate': nrm(ks[12], (GLA_HEADS * GLA_DK,), 0.1),
        'gla_norm_g': 1.0 + nrm(ks[13], (GLA_DV,), 0.02),
        'mla_q_norm_g': 1.0 + nrm(ks[14], (MLA_Q_LORA,), 0.02),
        'mla_kv_norm_g': 1.0 + nrm(ks[15], (MLA_KV_LORA,), 0.02),
        'w_uq': nrm(ks[16], (MLA_Q_LORA, MLA_HEADS * (MLA_NOPE + MLA_ROPE)), MLA_Q_LORA ** -0.5),
        'w_uk': nrm(ks[17], (MLA_KV_LORA, MLA_HEADS * MLA_NOPE), MLA_KV_LORA ** -0.5),
        'w_uv': nrm(ks[18], (MLA_KV_LORA, MLA_HEADS * MLA_DV), DN_BETA * MLA_KV_LORA ** -0.5),
        'w_o': nrm(ks[19], (MIX_WIDTH, D_MODEL), DN_BETA * MIX_WIDTH ** -0.5),
        'ln1_g': 1.0 + nrm(ks[20], (D_MODEL,), 0.02),
        'ln1_b': nrm(ks[21], (D_MODEL,), 0.02),
        'w_ffn_in': nrm(ks[22], (D_MODEL, 2 * D_FF), D_MODEL ** -0.5),
        'conv_w': nrm(ks[23], (CONV_W, D_FF), CONV_W ** -0.5),
        'conv_b': nrm(ks[24], (D_FF,), 0.02),
        'w_down': nrm(ks[25], (D_FF, D_MODEL), DN_BETA * D_FF ** -0.5),
        'ln2_g': 1.0 + nrm(ks[26], (D_MODEL,), 0.02),
        'ln2_b': nrm(ks[27], (D_MODEL,), 0.02),
    }


def reference(x_prompt, x_sample, cache_ckv, cache_krope, page_table, state_gla, state_conv,
              meta_tokens, ln_emb_g, ln_emb_b, w_in, w_gate_up, b_gate, gla_norm_g,
              mla_q_norm_g, mla_kv_norm_g, w_uq, w_uk, w_uv, w_o, ln1_g, ln1_b,
              w_ffn_in, conv_w, conv_b, w_down, ln2_g, ln2_b):
    proj = (w_in, w_gate_up, b_gate, mla_q_norm_g, mla_kv_norm_g, w_uq, w_uk)
    ffn_w = (w_ffn_in, conv_w, conv_b, w_down)

    L = N_META + SEQ
    meta = jnp.broadcast_to(meta_tokens.astype(x_prompt.dtype)[None], (BATCH, N_META, D_MODEL))
    h = layer_norm(jnp.concatenate([meta, x_prompt], axis=1), ln_emb_g, ln_emb_b)
    pos_p = jnp.arange(L, dtype=jnp.int32)
    gla_p = jnp.zeros((BATCH, GLA_HEADS, GLA_DK, GLA_DV), jnp.float32)
    ckv_prompt = None
    krope_prompt = None
    conv_prompt = None
    for _ in range(DEPTH):
        (q_g, k_g, v_g, log_a, g_g), (q_lat, q_rope, ckv, krope) = mixer_inputs(h, pos_p, *proj)
        o_meta, s_meta = gla_chunked(q_g[:, :N_META], k_g[:, :N_META], v_g[:, :N_META],
                                     log_a[:, :N_META], gla_p, N_META)
        o_tok, gla_p = gla_chunked(q_g[:, N_META:], k_g[:, N_META:], v_g[:, N_META:],
                                   log_a[:, N_META:], s_meta, GLA_CHUNK)
        o_gla = jnp.concatenate([o_meta, o_tok], axis=1).astype(h.dtype)
        o_lat = mla_prompt(q_lat, q_rope, ckv, krope)
        mix = mixer_output(o_gla, g_g, o_lat, gla_norm_g, w_uv, w_o)
        h1 = layer_norm(DN_ALPHA * h + mix, ln1_g, ln1_b)
        ffn, conv_prompt = conv_ffn(h1, jnp.zeros((BATCH, CONV_W - 1, D_FF), h1.dtype), *ffn_w)
        h = layer_norm(DN_ALPHA * h1 + ffn, ln2_g, ln2_b)
        ckv_prompt, krope_prompt = ckv, krope
    y_prompt = h[:, N_META:]

    hs = layer_norm(x_sample, ln_emb_g, ln_emb_b)
    pos_s = PAST_LEN + jnp.arange(DEC_SEQ, dtype=jnp.int32)
    gla_sample = state_gla
    conv_sample = state_conv
    ckv_sample = None
    krope_sample = None
    for _ in range(DEPTH):
        (q_g, k_g, v_g, log_a, g_g), (q_lat, q_rope, ckv, krope) = mixer_inputs(hs, pos_s, *proj)
        o_gla, gla_sample = gla_chunked(q_g, k_g, v_g, log_a, gla_sample, DEC_SEQ)
        o_lat = mla_sample(q_lat, q_rope, ckv, krope, cache_ckv, cache_krope, page_table)
        mix = mixer_output(o_gla.astype(hs.dtype), g_g, o_lat, gla_norm_g, w_uv, w_o)
        h1 = layer_norm(DN_ALPHA * hs + mix, ln1_g, ln1_b)
        ffn, conv_sample = conv_ffn(h1, conv_sample, *ffn_w)
        hs = layer_norm(DN_ALPHA * h1 + ffn, ln2_g, ln2_b)
        ckv_sample, krope_sample = ckv, krope
    y_sample = hs

    return (y_prompt, y_sample, ckv_prompt, krope_prompt, gla_p, conv_prompt,
            ckv_sample, krope_sample, gla_sample, conv_sample)
```

```python
import functools

import numpy as np
import jax
import jax.numpy as jnp
from jax import lax
from jax.experimental import pallas as pl
from jax.experimental.pallas import tpu as pltpu

F32 = jnp.float32
BF = jnp.bfloat16

D_MODEL = 1024
N_META = 16
GLA_HEADS = 4
GLA_DK = 64
GLA_DV = 128
GLA_RANK = 16
GLA_TAU = 16.0
GLA_CHUNK = 64
MLA_HEADS = 8
MLA_DV = 64
MLA_NOPE = 64
MLA_ROPE = 32
MLA_Q_LORA = 384
MLA_KV_LORA = 256
MLA_SCALE = (MLA_NOPE + MLA_ROPE) ** -0.5
ROPE_BASE = 10000.0
D_FF = 2816
DN_ALPHA = 2.0 ** 0.25
NORM_EPS = 1e-5
NEG_INF = -1e30
PAGE = 128

C_Q, C_K, C_V, C_G, C_CQ, C_CKV, C_LAST, C_END = 0, 256, 512, 1024, 1536, 1920, 2176, 2304
LAST_ALR = 64

VMEM_LIMIT = 56 * 1024 * 1024


def _dot(a, b):
    return jnp.dot(a, b, preferred_element_type=F32)


def _dot_nt(a, b):
    return lax.dot_general(a, b, (((1,), (1,)), ((), ())), preferred_element_type=F32)


def _dot_tn(a, b):
    return lax.dot_general(a, b, (((0,), (0,)), ((), ())), preferred_element_type=F32)


def _layer_norm(x, g, b):
    mu = jnp.mean(x, axis=-1, keepdims=True)
    xc = x - mu
    var = jnp.mean(xc * xc, axis=-1, keepdims=True)
    return xc * lax.rsqrt(var + NORM_EPS) * g + b


def _rms_norm(x, g):
    ms = jnp.mean(x * x, axis=-1, keepdims=True)
    return x * lax.rsqrt(ms + NORM_EPS) * g


def _silu(x):
    return x * (1.0 / (1.0 + jnp.exp(-x)))


def _proj_kernel(x_ref, lng_ref, lnb_ref, win_ref, wg_ref, bg_ref, qng_ref, kvg_ref, wuq_ref, wuk_ref,
                 cq_ref, sq_ref, ck_ref, sk_ref,
                 qg_ref, kg_ref, vg_ref, la_ref, gg_ref, qlat_ref, qr_ref, ckv_ref, kr_ref, kcat_ref):
    h = _layer_norm(x_ref[...], lng_ref[...], lnb_ref[...])
    z = _dot(h.astype(BF), win_ref[...])
    qg_ref[...] = z[:, C_Q:C_K] * (GLA_DK ** -0.5)
    kg_ref[...] = z[:, C_K:C_V]
    vg_ref[...] = z[:, C_V:C_G].astype(BF)
    gg_ref[...] = z[:, C_G:C_CQ]
    last = z[:, C_LAST:C_END]
    y = _dot(last.astype(BF), wg_ref[...]) + bg_ref[...]
    la_ref[...] = (jnp.minimum(y, 0.0) - jnp.log1p(jnp.exp(-jnp.abs(y)))) * (1.0 / GLA_TAU)
    cqn = _rms_norm(z[:, C_CQ:C_CKV], qng_ref[...])
    qa = _dot(cqn.astype(BF), wuq_ref[...])
    qn = (qa[:, 0:512] * MLA_SCALE).astype(BF)
    for hh in range(MLA_HEADS):
        pair = hh // 2
        qlat_ref[:, hh * 256:(hh + 1) * 256] = _dot(qn[:, pair * 128:(pair + 1) * 128], wuk_ref[hh]).astype(BF)
    qr = qa[:, 512:768] * cq_ref[...] + qa[:, 768:1024] * sq_ref[...]
    qr_ref[...] = (qr * MLA_SCALE).astype(BF)
    ckv = _rms_norm(z[:, C_CKV:C_LAST], kvg_ref[...])
    ckv_ref[...] = ckv
    kr = last * ck_ref[...] + pltpu.roll(last, 96, 1) * sk_ref[...]
    kr_ref[...] = kr[:, 0:MLA_ROPE]
    kr2 = kr + pltpu.roll(kr, 32, 1)
    kr4 = kr2 + pltpu.roll(kr2, 64, 1)
    kcat_ref[:, 0:256] = ckv.astype(BF)
    kcat_ref[:, 256:384] = kr4.astype(BF)
    kcat_ref[:, 384:512] = kr4.astype(BF)


def _proj(x, w, tabs, tm, tab_blocks):
    T = x.shape[0]
    nt = T // tm
    cq, sq, ck, sk = tabs
    row = lambda i: (i, 0)
    tab = (lambda i: (i % tab_blocks, 0)) if tab_blocks > 1 else (lambda i: (0, 0))
    c2 = lambda i: (0, 0)
    c3 = lambda i: (0, 0, 0)
    in_specs = [
        pl.BlockSpec((tm, D_MODEL), row),
        pl.BlockSpec((1, D_MODEL), c2), pl.BlockSpec((1, D_MODEL), c2),
        pl.BlockSpec((D_MODEL, C_END), c2),
        pl.BlockSpec((128, 256), c2), pl.BlockSpec((1, 256), c2),
        pl.BlockSpec((1, MLA_Q_LORA), c2), pl.BlockSpec((1, MLA_KV_LORA), c2),
        pl.BlockSpec((MLA_Q_LORA, 1024), c2),
        pl.BlockSpec((MLA_HEADS, 128, 256), c3),
        pl.BlockSpec((tm, 256), tab), pl.BlockSpec((tm, 256), tab),
        pl.BlockSpec((tm, 128), tab), pl.BlockSpec((tm, 128), tab),
    ]
    outs = [(256, F32), (256, F32), (512, BF), (256, F32), (512, F32), (2048, BF), (256, BF), (256, F32),
            (MLA_ROPE, F32), (512, BF)]
    out_shape = [jax.ShapeDtypeStruct((T, wd), dt) for wd, dt in outs]
    out_specs = [pl.BlockSpec((tm, wd), row) for wd, _ in outs]
    return pl.pallas_call(
        _proj_kernel, out_shape=out_shape, grid=(nt,), in_specs=in_specs, out_specs=out_specs,
        compiler_params=pltpu.CompilerParams(dimension_semantics=("parallel",), vmem_limit_bytes=VMEM_LIMIT),
        name="proj",
    )(x, w["lng"], w["lnb"], w["win"], w["wg"], w["bg"], w["qng"], w["kvg"], w["wuq"], w["wuk"], cq, sq, ck, sk)


def _gla_kernel(q_ref, k_ref, v_ref, la_ref, gg_ref, s0_ref, gn_ref, og_ref, sfin_ref, s_scr, *, C, n_chunks):
    i = pl.program_id(1)

    @pl.when(i == 0)
    def _():
        s_scr[...] = s0_ref[0]

    row = lax.broadcasted_iota(jnp.int32, (C, C), 0)
    col = lax.broadcasted_iota(jnp.int32, (C, C), 1)
    tril = col <= row
    tri_b = jnp.where(tril, 1.0, 0.0).astype(BF)
    eye = (lax.broadcasted_iota(jnp.int32, (GLA_DK, GLA_DK), 0)
           == lax.broadcasted_iota(jnp.int32, (GLA_DK, GLA_DK), 1))
    mid = C // 2
    gn = gn_ref[...]

    def chunk(c, carry):
        r = pl.ds(pl.multiple_of(c * C, C), C)
        g = la_ref[r, :]
        g1 = g.astype(BF)
        r1 = g - g1.astype(F32)
        g2 = r1.astype(BF)
        g3 = (r1 - g2.astype(F32)).astype(BF)
        b = _dot(tri_b, g1) + _dot(tri_b, g2) + _dot(tri_b, g3)
        b_mid = b[mid:mid + 1, :]
        b_last = b[C - 1:C, :]
        q = q_ref[r, :]
        k = k_ref[r, :]
        qe = (q * jnp.exp(b)).astype(BF)
        qm = (q * jnp.exp(b - b_mid)).astype(BF)
        km = (k * jnp.exp(b_mid - b)).astype(BF)
        kl = (k * jnp.exp(b_last - b)).astype(BF)
        dec = jnp.exp(b_last)
        for hh in range(GLA_HEADS):
            ks = slice(hh * GLA_DK, (hh + 1) * GLA_DK)
            vs = slice(hh * GLA_DV, (hh + 1) * GLA_DV)
            s = s_scr[hh]
            v = v_ref[r, vs]
            o = _dot(qe[:, ks], s.astype(BF))
            att = jnp.where(tril, _dot_nt(qm[:, ks], km[:, ks]), 0.0)
            o = o + _dot(att.astype(BF), v)
            dcol = jnp.sum(jnp.where(eye, jnp.broadcast_to(dec[:, ks], (GLA_DK, GLA_DK)), 0.0),
                           axis=1, keepdims=True)
            s_scr[hh] = s * dcol + _dot_tn(kl[:, ks], v)
            on = _rms_norm(o, gn)
            og_ref[r, vs] = (on * _silu(gg_ref[r, vs])).astype(BF)
        return carry

    lax.fori_loop(0, n_chunks, chunk, 0)

    @pl.when(i == pl.num_programs(1) - 1)
    def _():
        sfin_ref[0] = s_scr[...]


def _gla(q, k, v, la, gg, s0, gn, *, B, L, C, tg, s0_per_batch):
    nt = L // tg
    row = lambda b, i: (b * nt + i, 0)
    s0map = (lambda b, i: (b, 0, 0, 0)) if s0_per_batch else (lambda b, i: (0, 0, 0, 0))
    st = (1, GLA_HEADS, GLA_DK, GLA_DV)
    in_specs = [pl.BlockSpec((tg, 256), row), pl.BlockSpec((tg, 256), row), pl.BlockSpec((tg, 512), row),
                pl.BlockSpec((tg, 256), row), pl.BlockSpec((tg, 512), row),
                pl.BlockSpec(st, s0map), pl.BlockSpec((1, GLA_DV), lambda b, i: (0, 0))]
    out_shape = [jax.ShapeDtypeStruct((B * L, 512), BF), jax.ShapeDtypeStruct((B,) + st[1:], F32)]
    out_specs = [pl.BlockSpec((tg, 512), row), pl.BlockSpec(st, lambda b, i: (b, 0, 0, 0))]
    return pl.pallas_call(
        functools.partial(_gla_kernel, C=C, n_chunks=tg // C),
        out_shape=out_shape, grid=(B, nt), in_specs=in_specs, out_specs=out_specs,
        scratch_shapes=[pltpu.VMEM(st[1:], F32)],
        compiler_params=pltpu.CompilerParams(dimension_semantics=("parallel", "arbitrary"),
                                             vmem_limit_bytes=VMEM_LIMIT),
        name="gla_c%d" % C,
    )(q, k, v, la, gg, s0, gn)


def _softmax_step(s, m, l, acc, vb):
    m_new = jnp.maximum(m, jnp.max(s, axis=1, keepdims=True))
    alpha = jnp.exp(m - m_new)
    p = jnp.exp(s - m_new)
    l_new = alpha * l + jnp.sum(p, axis=1, keepdims=True)
    acc_new = alpha * acc + _dot(p.astype(BF), vb)
    return m_new, l_new, acc_new


def _attn_kernel(qlat_ref, qr_ref, kcat_ref, *rest, tq, has_meta):
    if has_meta:
        kmeta_ref, o_ref, q_scr, m_scr, l_scr, acc_scr = rest
    else:
        o_ref, q_scr, m_scr, l_scr, acc_scr = rest
    i = pl.program_id(1)
    R = MLA_HEADS * tq
    lane_head = lax.broadcasted_iota(jnp.int32, (tq, 256), 1) // MLA_ROPE
    qr = qr_ref[...]
    for hh in range(MLA_HEADS):
        rows = slice(hh * tq, (hh + 1) * tq)
        q_scr[rows, 0:256] = qlat_ref[:, hh * 256:(hh + 1) * 256]
        q_scr[rows, 256:512] = jnp.where(lane_head == hh, qr, jnp.zeros_like(qr))
    q = q_scr[...]

    kd = kcat_ref[pl.ds(pl.multiple_of(i * tq, tq), tq), :]
    s = _dot_nt(q, kd)
    tok = lax.broadcasted_iota(jnp.int32, (R, tq), 0) % tq
    key = lax.broadcasted_iota(jnp.int32, (R, tq), 1)
    s = jnp.where(key <= tok, s, NEG_INF)
    m = jnp.max(s, axis=1, keepdims=True)
    p = jnp.exp(s - m)
    l = jnp.sum(p, axis=1, keepdims=True)
    acc = _dot(p.astype(BF), kd[:, 0:256])
    if has_meta:
        km = kmeta_ref[...]
        m, l, acc = _softmax_step(_dot_nt(q, km), m, l, acc, km[:, 0:256])
    m_scr[...] = m
    l_scr[...] = l
    acc_scr[...] = acc

    def full(j, carry):
        kc = kcat_ref[pl.ds(pl.multiple_of(j * tq, tq), tq), :]
        mj, lj, aj = _softmax_step(_dot_nt(q_scr[...], kc), m_scr[...], l_scr[...], acc_scr[...], kc[:, 0:256])
        m_scr[...] = mj
        l_scr[...] = lj
        acc_scr[...] = aj
        return carry

    lax.fori_loop(0, i, full, 0)
    o = acc_scr[...] * (1.0 / l_scr[...])
    for hh in range(MLA_HEADS):
        o_ref[:, hh * 256:(hh + 1) * 256] = o[hh * tq:(hh + 1) * tq, :].astype(BF)


def _attn(qlat, qr, kcat, kmeta, *, B, L, tq):
    nq = L // tq
    row = lambda b, i: (b * nq + i, 0)
    in_specs = [pl.BlockSpec((tq, 2048), row), pl.BlockSpec((tq, 256), row),
                pl.BlockSpec((L, 512), lambda b, i: (b, 0))]
    args = [qlat, qr, kcat]
    if kmeta is not None:
        in_specs.append(pl.BlockSpec((N_META, 512), lambda b, i: (0, 0)))
        args.append(kmeta)
    R = MLA_HEADS * tq
    return pl.pallas_call(
        functools.partial(_attn_kernel, tq=tq, has_meta=kmeta is not None),
        out_shape=jax.ShapeDtypeStruct((B * L, 2048), BF), grid=(B, nq), in_specs=in_specs,
        out_specs=pl.BlockSpec((tq, 2048), row),
        scratch_shapes=[pltpu.VMEM((R, 512), BF), pltpu.VMEM((R, 1), F32), pltpu.VMEM((R, 1), F32),
                        pltpu.VMEM((R, 256), F32)],
        compiler_params=pltpu.CompilerParams(dimension_semantics=("parallel", "arbitrary"),
                                             vmem_limit_bytes=VMEM_LIMIT),
        name="attn_t%d" % tq,
    )(*args)


PG = 8
NEW_PAD = 16


def _pattn_kernel(pt_ref, ql_ref, qr_ref, cn_ref, kn_ref, cckv_hbm, ckr_hbm, o_ref, ckv_buf, kr_buf, sem,
                  *, n_pages, t_new):
    b = pl.program_id(0)
    n_chunks = n_pages // PG

    def copies(ch, slot):
        out = []
        for pg in range(PG):
            page = pt_ref[b, ch * PG + pg]
            out.append(pltpu.make_async_copy(cckv_hbm.at[page], ckv_buf.at[slot, pg], sem.at[0, slot]))
            out.append(pltpu.make_async_copy(ckr_hbm.at[page], kr_buf.at[slot, pg], sem.at[1, slot]))
        return out

    ql = ql_ref[0]
    qr = qr_ref[0]
    R = ql.shape[0]
    cn = cn_ref[0].astype(BF)
    kn = kn_ref[0].astype(BF)
    s = _dot_nt(ql, cn) + _dot_nt(qr, kn)
    tok = lax.broadcasted_iota(jnp.int32, (R, NEW_PAD), 0) % t_new
    key = lax.broadcasted_iota(jnp.int32, (R, NEW_PAD), 1)
    s = jnp.where(key <= tok, s, NEG_INF)
    m = jnp.max(s, axis=1, keepdims=True)
    p = jnp.exp(s - m)
    l = jnp.sum(p, axis=1, keepdims=True)
    acc = _dot(p.astype(BF), cn)

    for c in copies(0, 0):
        c.start()
    for ch in range(n_chunks):
        slot = ch % 2
        if ch + 1 < n_chunks:
            for c in copies(ch + 1, 1 - slot):
                c.start()
        for c in copies(ch, slot):
            c.wait()
        kc = ckv_buf[slot].reshape(PG * PAGE, MLA_KV_LORA).astype(BF)
        kr = kr_buf[slot].reshape(PG * PAGE, MLA_ROPE).astype(BF)
        m, l, acc = _softmax_step(_dot_nt(ql, kc) + _dot_nt(qr, kr), m, l, acc, kc)
    o_ref[0] = acc * (1.0 / l)


def _pattn(page_table, ql, qr, cn, kn, cache_ckv, cache_krope, *, t_new):
    Bd, n_pages = page_table.shape
    R = ql.shape[1]
    grid_spec = pltpu.PrefetchScalarGridSpec(
        num_scalar_prefetch=1, grid=(Bd,),
        in_specs=[pl.BlockSpec((1, R, MLA_KV_LORA), lambda b, pt: (b, 0, 0)),
                  pl.BlockSpec((1, R, MLA_ROPE), lambda b, pt: (b, 0, 0)),
                  pl.BlockSpec((1, NEW_PAD, MLA_KV_LORA), lambda b, pt: (b, 0, 0)),
                  pl.BlockSpec((1, NEW_PAD, MLA_ROPE), lambda b, pt: (b, 0, 0)),
                  pl.BlockSpec(memory_space=pl.ANY), pl.BlockSpec(memory_space=pl.ANY)],
        out_specs=pl.BlockSpec((1, R, MLA_KV_LORA), lambda b, pt: (b, 0, 0)),
        scratch_shapes=[pltpu.VMEM((2, PG, PAGE, MLA_KV_LORA), F32), pltpu.VMEM((2, PG, PAGE, MLA_ROPE), F32),
                        pltpu.SemaphoreType.DMA((2, 2))])
    return pl.pallas_call(
        functools.partial(_pattn_kernel, n_pages=n_pages, t_new=t_new),
        out_shape=jax.ShapeDtypeStruct((Bd, R, MLA_KV_LORA), F32), grid_spec=grid_spec,
        compiler_params=pltpu.CompilerParams(dimension_semantics=("arbitrary",), vmem_limit_bytes=VMEM_LIMIT),
        name="pattn",
    )(page_table, ql, qr, cn, kn, cache_ckv, cache_krope)


FF_CH = 256
N_FF_CH = D_FF // FF_CH


def _ffn_kernel(x_ref, og_ref, ol_ref, lng_ref, lnb_ref, wuv_ref, wo_ref, l1g_ref, l1b_ref, wa_ref, wu_ref,
                cw_ref, cb_ref, wd_ref, l2g_ref, l2b_ref, *rest, tm, inject):
    if inject:
        inj1_ref, inj2_ref, m1_ref, m2_ref, y_ref, a_ref = rest
    else:
        halo0_ref, y_ref, alast_ref, halo_scr = rest
        i = pl.program_id(1)

        @pl.when(i == 0)
        def _():
            halo_scr[...] = halo0_ref[...]

    h = _layer_norm(x_ref[...], lng_ref[...], lnb_ref[...])
    om = jnp.concatenate([_dot(ol_ref[:, j * 512:(j + 1) * 512], wuv_ref[j]) for j in range(MLA_HEADS // 2)],
                         axis=1)
    mix = _dot(og_ref[...], wo_ref[0:512, :]) + _dot(om.astype(BF), wo_ref[512:1024, :])
    h1 = _layer_norm(DN_ALPHA * h + mix, l1g_ref[...], l1b_ref[...])
    h1b = h1.astype(BF)
    ridx = lax.broadcasted_iota(jnp.int32, (tm, FF_CH), 0)
    acc = jnp.zeros((tm, D_MODEL), F32)
    for c in range(N_FF_CH):
        cs = slice(c * FF_CH, (c + 1) * FF_CH)
        a = _dot(h1b, wa_ref[:, cs])
        up = _dot(h1b, wu_ref[:, cs])
        r1 = pltpu.roll(a, 1, 0)
        r2 = pltpu.roll(a, 2, 0)
        if inject:
            a1 = jnp.where(m1_ref[...] > 0.0, inj1_ref[:, cs], r1)
            a2 = jnp.where(m2_ref[...] > 0.0, inj2_ref[:, cs], r2)
            a_ref[:, cs] = a
        else:
            hl = halo_scr[:, cs]
            a1 = jnp.where(ridx == 0, hl[1:2, :], r1)
            a2 = jnp.where(ridx == 0, hl[0:1, :], jnp.where(ridx == 1, hl[1:2, :], r2))
            halo_scr[:, cs] = a[tm - 2:tm, :]
            alast_ref[0, :, cs] = a[tm - 2:tm, :]
        conv = cb_ref[:, cs] + cw_ref[0:1, cs] * a2 + cw_ref[1:2, cs] * a1 + cw_ref[2:3, cs] * a
        act = (_silu(conv) * up).astype(BF)
        acc = acc + _dot(act, wd_ref[cs, :])
    y_ref[...] = _layer_norm(DN_ALPHA * h1 + acc, l2g_ref[...], l2b_ref[...])


def _ffn(x, og, ol, w, *, B, L, tm, halo0=None, inject=None):
    nt = L // tm
    T = B * L
    row = lambda b, i: (b * nt + i, 0)
    c2 = lambda b, i: (0, 0)
    c3 = lambda b, i: (0, 0, 0)
    once = pl.Buffered(1)
    in_specs = [pl.BlockSpec((tm, D_MODEL), row), pl.BlockSpec((tm, 512), row), pl.BlockSpec((tm, 2048), row),
                pl.BlockSpec((1, D_MODEL), c2), pl.BlockSpec((1, D_MODEL), c2),
                pl.BlockSpec((MLA_HEADS // 2, 512, 128), c3, pipeline_mode=once),
                pl.BlockSpec((D_MODEL, D_MODEL), c2, pipeline_mode=once),
                pl.BlockSpec((1, D_MODEL), c2), pl.BlockSpec((1, D_MODEL), c2),
                pl.BlockSpec((D_MODEL, D_FF), c2, pipeline_mode=once),
                pl.BlockSpec((D_MODEL, D_FF), c2, pipeline_mode=once),
                pl.BlockSpec((3, D_FF), c2), pl.BlockSpec((1, D_FF), c2),
                pl.BlockSpec((D_FF, D_MODEL), c2, pipeline_mode=once),
                pl.BlockSpec((1, D_MODEL), c2), pl.BlockSpec((1, D_MODEL), c2)]
    args = [x, og, ol, w["lng"], w["lnb"], w["wuv"], w["wo"], w["l1g"], w["l1b"], w["wa"], w["wu"],
            w["cw"], w["cb"], w["wd"], w["l2g"], w["l2b"]]
    scratch = []
    if inject is not None:
        inj1, inj2, m1, m2 = inject
        in_specs += [pl.BlockSpec((tm, D_FF), row), pl.BlockSpec((tm, D_FF), row),
                     pl.BlockSpec((tm, 1), row), pl.BlockSpec((tm, 1), row)]
        args += [inj1, inj2, m1, m2]
        out_shape = [jax.ShapeDtypeStruct((T, D_MODEL), F32), jax.ShapeDtypeStruct((T, D_FF), F32)]
        out_specs = [pl.BlockSpec((tm, D_MODEL), row), pl.BlockSpec((tm, D_FF), row)]
    else:
        in_specs.append(pl.BlockSpec((2, D_FF), c2))
        args.append(halo0)
        out_shape = [jax.ShapeDtypeStruct((T, D_MODEL), F32), jax.ShapeDtypeStruct((B, 2, D_FF), F32)]
        out_specs = [pl.BlockSpec((tm, D_MODEL), row), pl.BlockSpec((1, 2, D_FF), lambda b, i: (b, 0, 0))]
        scratch = [pltpu.VMEM((2, D_FF), F32)]
    return pl.pallas_call(
        functools.partial(_ffn_kernel, tm=tm, inject=inject is not None),
        out_shape=out_shape, grid=(B, nt), in_specs=in_specs, out_specs=out_specs, scratch_shapes=scratch,
        compiler_params=pltpu.CompilerParams(dimension_semantics=("parallel", "arbitrary"),
                                             vmem_limit_bytes=VMEM_LIMIT),
        name="ffn_inject" if inject is not None else "ffn_seq",
    )(*args)


def _rope_tables(pos):
    half = MLA_ROPE // 2
    inv = ROPE_BASE ** (-jnp.arange(half, dtype=F32) / half)
    ang = pos.astype(F32)[:, None] * inv[None, :]
    cos = jnp.concatenate([jnp.cos(ang)] * 2, axis=1)
    sin = jnp.concatenate([jnp.sin(ang)] * 2, axis=1)
    pad = jnp.zeros((pos.shape[0], 128 - MLA_ROPE), F32)
    return (jnp.tile(cos, (1, MLA_HEADS)), jnp.tile(sin, (1, MLA_HEADS)),
            jnp.concatenate([cos, pad], axis=1), jnp.concatenate([sin, pad], axis=1))


def _swap_halves(w):
    half = w.shape[-1] // 2
    return jnp.concatenate([-w[..., half:], w[..., :half]], axis=-1)


def _prep_weights(ln_emb_g, ln_emb_b, w_in, w_gate_up, b_gate, gla_norm_g, mla_q_norm_g, mla_kv_norm_g,
                  w_uq, w_uk, w_uv, w_o, ln1_g, ln1_b, w_ffn_in, conv_w, conv_b, w_down, ln2_g, ln2_b):
    o_alr, o_cq, o_ckv, o_kr = 1536, 1552, 1936, 2192
    w_kr = w_in[:, o_kr:o_kr + MLA_ROPE]
    last = jnp.concatenate([w_kr, _swap_halves(w_kr), w_in[:, o_alr:o_cq],
                            jnp.zeros((D_MODEL, 128 - 2 * MLA_ROPE - GLA_RANK), F32)], axis=1)
    win = jnp.concatenate([w_in[:, :o_alr], w_in[:, o_cq:o_kr], last], axis=1).astype(BF)
    wg = jnp.zeros((128, 256), F32).at[LAST_ALR:LAST_ALR + GLA_RANK].set(w_gate_up).astype(BF)
    uq = w_uq.reshape(MLA_Q_LORA, MLA_HEADS, MLA_NOPE + MLA_ROPE)
    uq_r = uq[:, :, MLA_NOPE:]
    wuq = jnp.concatenate([uq[:, :, :MLA_NOPE].reshape(MLA_Q_LORA, 512), uq_r.reshape(MLA_Q_LORA, 256),
                           _swap_halves(uq_r).reshape(MLA_Q_LORA, 256)], axis=1).astype(BF)
    ukt = jnp.transpose(w_uk.reshape(MLA_KV_LORA, MLA_HEADS, MLA_NOPE), (1, 2, 0))
    z64 = jnp.zeros_like(ukt)
    even = (jnp.arange(MLA_HEADS) % 2 == 0)[:, None, None]
    wuk = jnp.concatenate([jnp.where(even, ukt, z64), jnp.where(even, z64, ukt)], axis=1).astype(BF)
    uv = jnp.transpose(w_uv.reshape(MLA_KV_LORA, MLA_HEADS, MLA_DV), (1, 0, 2))
    uv = uv.reshape(MLA_HEADS // 2, 2, MLA_KV_LORA, MLA_DV)
    zz = jnp.zeros_like(uv[:, 0])
    wuv = jnp.concatenate([jnp.concatenate([uv[:, 0], zz], axis=2), jnp.concatenate([zz, uv[:, 1]], axis=2)],
                          axis=1).astype(BF)
    r = lambda a: a.reshape(1, -1)
    return dict(lng=r(ln_emb_g), lnb=r(ln_emb_b), win=win, wg=wg, bg=r(b_gate), qng=r(mla_q_norm_g),
                kvg=r(mla_kv_norm_g), wuq=wuq, wuk=wuk, wuv=wuv, wo=w_o.astype(BF), l1g=r(ln1_g), l1b=r(ln1_b),
                wa=w_ffn_in[:, :D_FF].astype(BF), wu=w_ffn_in[:, D_FF:].astype(BF), cw=conv_w, cb=r(conv_b),
                wd=w_down.astype(BF), l2g=r(ln2_g), l2b=r(ln2_b), gn=r(gla_norm_g))


def kernel(x_prompt, x_sample, cache_ckv, cache_krope, page_table, state_gla, state_conv, meta_tokens, ln_emb_g, ln_emb_b, w_in, w_gate_up, b_gate, gla_norm_g, mla_q_norm_g, mla_kv_norm_g, w_uq, w_uk, w_uv, w_o, ln1_g, ln1_b, w_ffn_in, conv_w, conv_b, w_down, ln2_g, ln2_b):
    B, S, _ = x_prompt.shape
    Bd, Td, _ = x_sample.shape
    past_len = page_table.shape[1] * PAGE
    n_small = N_META + Bd * Td
    w = _prep_weights(ln_emb_g, ln_emb_b, w_in, w_gate_up, b_gate, gla_norm_g, mla_q_norm_g, mla_kv_norm_g,
                      w_uq, w_uk, w_uv, w_o, ln1_g, ln1_b, w_ffn_in, conv_w, conv_b, w_down, ln2_g, ln2_b)

    xs = jnp.concatenate([meta_tokens.astype(F32), x_sample.reshape(Bd * Td, D_MODEL)], axis=0)
    pos_s = jnp.concatenate([jnp.arange(N_META, dtype=jnp.int32),
                             jnp.tile(past_len + jnp.arange(Td, dtype=jnp.int32), Bd)])
    (qg_s, kg_s, vg_s, la_s, gg_s, qlat_s, qr_s, ckv_s, kr_s, kcat_s) = _proj(
        xs, w, _rope_tables(pos_s), n_small, 1)

    M = N_META
    zero_state = jnp.zeros((1, GLA_HEADS, GLA_DK, GLA_DV), F32)
    og_m, s_meta = _gla(qg_s[:M], kg_s[:M], vg_s[:M], la_s[:M], gg_s[:M], zero_state, w["gn"],
                        B=1, L=M, C=M, tg=M, s0_per_batch=False)
    ol_m = _attn(qlat_s[:M], qr_s[:M], kcat_s[:M], None, B=1, L=M, tq=M)

    TP = 16
    padt = lambda a: jnp.pad(a[M:].reshape(Bd, Td, -1), ((0, 0), (0, TP - Td), (0, 0))).reshape(Bd * TP, -1)
    og_sp, gla_sample = _gla(padt(qg_s), padt(kg_s), padt(vg_s), padt(la_s), padt(gg_s), state_gla, w["gn"],
                             B=Bd, L=TP, C=TP, tg=TP, s0_per_batch=True)
    og_s = og_sp.reshape(Bd, TP, 512)[:, :Td].reshape(Bd * Td, 512)

    heads_first = lambda a, d: jnp.transpose(a[M:].reshape(Bd, Td, MLA_HEADS, d), (0, 2, 1, 3)).reshape(
        Bd, MLA_HEADS * Td, d)
    padn = lambda a: jnp.pad(a[M:].reshape(Bd, Td, -1), ((0, 0), (0, NEW_PAD - Td), (0, 0)))
    o_s = _pattn(page_table, heads_first(qlat_s, MLA_KV_LORA), heads_first(qr_s, MLA_ROPE),
                 padn(ckv_s), padn(kr_s), cache_ckv, cache_krope, t_new=Td)
    ol_s = jnp.transpose(o_s.reshape(Bd, MLA_HEADS, Td, MLA_KV_LORA), (0, 2, 1, 3)).reshape(
        Bd * Td, MLA_HEADS * MLA_KV_LORA).astype(BF)

    sc = state_conv.astype(F32)
    zrow = jnp.zeros((Bd, 1, D_FF), F32)
    inj1 = jnp.concatenate([sc[:, 1:2], zrow, zrow, zrow], axis=1).reshape(Bd * Td, D_FF)
    inj2 = jnp.concatenate([sc[:, 0:1], sc[:, 1:2], zrow, zrow], axis=1).reshape(Bd * Td, D_FF)
    zmeta = jnp.zeros((M, D_FF), F32)
    t_in = np.concatenate([np.arange(M), np.tile(np.arange(Td), Bd)])
    m1 = jnp.asarray((t_in == 0).astype(np.float32)[:, None])
    m2 = jnp.asarray((t_in <= 1).astype(np.float32)[:, None])
    y_s, a_s = _ffn(xs, jnp.concatenate([og_m, og_s], axis=0), jnp.concatenate([ol_m, ol_s], axis=0), w,
                    B=1, L=n_small, tm=176,
                    inject=(jnp.concatenate([zmeta, inj1], axis=0), jnp.concatenate([zmeta, inj2], axis=0), m1, m2))

    TM = 512
    xp = x_prompt.reshape(B * S, D_MODEL)
    pos_p = N_META + jnp.arange(S, dtype=jnp.int32)
    (qg_p, kg_p, vg_p, la_p, gg_p, qlat_p, qr_p, ckv_p, kr_p, kcat_p) = _proj(
        xp, w, _rope_tables(pos_p), TM, S // TM)
    og_p, gla_p = _gla(qg_p, kg_p, vg_p, la_p, gg_p, s_meta, w["gn"], B=B, L=S, C=GLA_CHUNK, tg=TM,
                       s0_per_batch=False)
    ol_p = _attn(qlat_p, qr_p, kcat_p, kcat_s[:M], B=B, L=S, tq=256)
    y_p, conv_prompt = _ffn(xp, og_p, ol_p, w, B=B, L=S, tm=TM, halo0=a_s[M - 2:M])

    y_prompt = y_p.reshape(B, S, D_MODEL)
    y_sample = y_s[M:].reshape(Bd, Td, D_MODEL)
    ckv_prompt = jnp.concatenate([jnp.broadcast_to(ckv_s[None, :M], (B, M, MLA_KV_LORA)),
                                  ckv_p.reshape(B, S, MLA_KV_LORA)], axis=1)
    krope_prompt = jnp.concatenate([jnp.broadcast_to(kr_s[None, :M], (B, M, MLA_ROPE)),
                                    kr_p.reshape(B, S, MLA_ROPE)], axis=1)
    ckv_sample = ckv_s[M:].reshape(Bd, Td, MLA_KV_LORA)
    krope_sample = kr_s[M:].reshape(Bd, Td, MLA_ROPE)
    conv_sample = a_s[M:].reshape(Bd, Td, D_FF)[:, Td - 2:]
    return (y_prompt, y_sample, ckv_prompt, krope_prompt, gla_p, conv_prompt,
            ckv_sample, krope_sample, gla_sample, conv_sample)
```

```python
import functools

import numpy as np
import jax
import jax.numpy as jnp
from jax import lax
from jax.experimental import pallas as pl
from jax.experimental.pallas import tpu as pltpu

F32 = jnp.float32
BF = jnp.bfloat16

D_MODEL = 1024
N_META = 16
GLA_HEADS = 4
GLA_DK = 64
GLA_DV = 128
GLA_RANK = 16
GLA_TAU = 16.0
GLA_CHUNK = 64
MLA_HEADS = 8
MLA_DV = 64
MLA_NOPE = 64
MLA_ROPE = 32
MLA_Q_LORA = 384
MLA_KV_LORA = 256
MLA_SCALE = (MLA_NOPE + MLA_ROPE) ** -0.5
Q_SCALE = MLA_SCALE * 1.4426950408889634
ROPE_BASE = 10000.0
D_FF = 2816
DN_ALPHA = 2.0 ** 0.25
NORM_EPS = 1e-5
NEG_INF = -1e30
PAGE = 128

C_Q, C_K, C_V, C_G, C_CQ, C_CKV, C_LAST, C_END = 0, 256, 512, 1024, 1536, 1920, 2176, 2304
LAST_ALR = 64

VMEM_LIMIT = 56 * 1024 * 1024


def _dot(a, b):
    return jnp.dot(a, b, preferred_element_type=F32)


def _dot_nt(a, b):
    return lax.dot_general(a, b, (((1,), (1,)), ((), ())), preferred_element_type=F32)


def _dot_tn(a, b):
    return lax.dot_general(a, b, (((0,), (0,)), ((), ())), preferred_element_type=F32)


def _layer_norm(x, g, b):
    mu = jnp.mean(x, axis=-1, keepdims=True)
    xc = x - mu
    var = jnp.mean(xc * xc, axis=-1, keepdims=True)
    return xc * lax.rsqrt(var + NORM_EPS) * g + b


def _rms_norm(x, g):
    ms = jnp.mean(x * x, axis=-1, keepdims=True)
    return x * lax.rsqrt(ms + NORM_EPS) * g


def _silu(x):
    return x * (1.0 / (1.0 + jnp.exp(-x)))


def _proj_kernel(x_ref, lng_ref, lnb_ref, win_ref, wg_ref, bg_ref, qng_ref, kvg_ref, wuq_ref, wuk_ref,
                 cq_ref, sq_ref, ck_ref, sk_ref,
                 qg_ref, kg_ref, vg_ref, la_ref, gg_ref, qlat_ref, qr_ref, ckv_ref, kr_ref, kcat_ref):
    h = _layer_norm(x_ref[...], lng_ref[...], lnb_ref[...])
    z = _dot(h.astype(BF), win_ref[...])
    qg_ref[...] = z[:, C_Q:C_K] * (GLA_DK ** -0.5)
    kg_ref[...] = z[:, C_K:C_V]
    vg_ref[...] = z[:, C_V:C_G].astype(BF)
    gg_ref[...] = z[:, C_G:C_CQ]
    last = z[:, C_LAST:C_END]
    y = _dot(last.astype(BF), wg_ref[...]) + bg_ref[...]
    la_ref[...] = (jnp.minimum(y, 0.0) - jnp.log1p(jnp.exp(-jnp.abs(y)))) * (1.0 / GLA_TAU)
    cqn = _rms_norm(z[:, C_CQ:C_CKV], qng_ref[...])
    qa = _dot(cqn.astype(BF), wuq_ref[...])
    qn = (qa[:, 0:512] * Q_SCALE).astype(BF)
    for hh in range(MLA_HEADS):
        pair = hh // 2
        qlat_ref[:, hh * 256:(hh + 1) * 256] = _dot(qn[:, pair * 128:(pair + 1) * 128], wuk_ref[hh]).astype(BF)
    qr = qa[:, 512:768] * cq_ref[...] + qa[:, 768:1024] * sq_ref[...]
    qr_ref[...] = (qr * Q_SCALE).astype(BF)
    ckv = _rms_norm(z[:, C_CKV:C_LAST], kvg_ref[...])
    ckv_ref[...] = ckv
    kr = last * ck_ref[...] + pltpu.roll(last, 96, 1) * sk_ref[...]
    kr_ref[...] = kr[:, 0:MLA_ROPE]
    kr2 = kr + pltpu.roll(kr, 32, 1)
    kr4 = kr2 + pltpu.roll(kr2, 64, 1)
    kcat_ref[:, 0:256] = ckv.astype(BF)
    kcat_ref[:, 256:384] = kr4.astype(BF)
    kcat_ref[:, 384:512] = kr4.astype(BF)


def _proj(x, w, tabs, tm, tab_blocks):
    T = x.shape[0]
    nt = T // tm
    cq, sq, ck, sk = tabs
    row = lambda i: (i, 0)
    tab = (lambda i: (i % tab_blocks, 0)) if tab_blocks > 1 else (lambda i: (0, 0))
    c2 = lambda i: (0, 0)
    c3 = lambda i: (0, 0, 0)
    in_specs = [
        pl.BlockSpec((tm, D_MODEL), row),
        pl.BlockSpec((1, D_MODEL), c2), pl.BlockSpec((1, D_MODEL), c2),
        pl.BlockSpec((D_MODEL, C_END), c2),
        pl.BlockSpec((128, 256), c2), pl.BlockSpec((1, 256), c2),
        pl.BlockSpec((1, MLA_Q_LORA), c2), pl.BlockSpec((1, MLA_KV_LORA), c2),
        pl.BlockSpec((MLA_Q_LORA, 1024), c2),
        pl.BlockSpec((MLA_HEADS, 128, 256), c3),
        pl.BlockSpec((tm, 256), tab), pl.BlockSpec((tm, 256), tab),
        pl.BlockSpec((tm, 128), tab), pl.BlockSpec((tm, 128), tab),
    ]
    outs = [(256, F32), (256, F32), (512, BF), (256, F32), (512, F32), (2048, BF), (256, BF), (256, F32),
            (MLA_ROPE, F32), (512, BF)]
    out_shape = [jax.ShapeDtypeStruct((T, wd), dt) for wd, dt in outs]
    out_specs = [pl.BlockSpec((tm, wd), row) for wd, _ in outs]
    return pl.pallas_call(
        _proj_kernel, out_shape=out_shape, grid=(nt,), in_specs=in_specs, out_specs=out_specs,
        compiler_params=pltpu.CompilerParams(dimension_semantics=("parallel",), vmem_limit_bytes=VMEM_LIMIT),
        name="proj",
    )(x, w["lng"], w["lnb"], w["win"], w["wg"], w["bg"], w["qng"], w["kvg"], w["wuq"], w["wuk"], cq, sq, ck, sk)


def _gla_kernel(q_ref, k_ref, v_ref, la_ref, gg_ref, s0_ref, gn_ref, og_ref, sfin_ref, s_scr, *, C, n_chunks):
    i = pl.program_id(1)

    @pl.when(i == 0)
    def _():
        s_scr[...] = s0_ref[0]

    row = lax.broadcasted_iota(jnp.int32, (C, C), 0)
    col = lax.broadcasted_iota(jnp.int32, (C, C), 1)
    tril = col <= row
    tri_b = jnp.where(tril, 1.0, 0.0).astype(BF)
    eye = (lax.broadcasted_iota(jnp.int32, (GLA_DK, GLA_DK), 0)
           == lax.broadcasted_iota(jnp.int32, (GLA_DK, GLA_DK), 1))
    mid = C // 2
    gn = gn_ref[...]

    def chunk(c, carry):
        r = pl.ds(pl.multiple_of(c * C, C), C)
        g = la_ref[r, :]
        g1 = g.astype(BF)
        r1 = g - g1.astype(F32)
        g2 = r1.astype(BF)
        g3 = (r1 - g2.astype(F32)).astype(BF)
        b = _dot(tri_b, g1) + _dot(tri_b, g2) + _dot(tri_b, g3)
        b_mid = b[mid:mid + 1, :]
        b_last = b[C - 1:C, :]
        q = q_ref[r, :]
        k = k_ref[r, :]
        qe = (q * jnp.exp(b)).astype(BF)
        qm = (q * jnp.exp(b - b_mid)).astype(BF)
        km = (k * jnp.exp(b_mid - b)).astype(BF)
        kl = (k * jnp.exp(b_last - b)).astype(BF)
        dec = jnp.exp(b_last)
        for hh in range(GLA_HEADS):
            ks = slice(hh * GLA_DK, (hh + 1) * GLA_DK)
            vs = slice(hh * GLA_DV, (hh + 1) * GLA_DV)
            s = s_scr[hh]
            v = v_ref[r, vs]
            o = _dot(qe[:, ks], s.astype(BF))
            att = jnp.where(tril, _dot_nt(qm[:, ks], km[:, ks]), 0.0)
            o = o + _dot(att.astype(BF), v)
            dcol = jnp.sum(jnp.where(eye, jnp.broadcast_to(dec[:, ks], (GLA_DK, GLA_DK)), 0.0),
                           axis=1, keepdims=True)
            s_scr[hh] = s * dcol + _dot_tn(kl[:, ks], v)
            on = _rms_norm(o, gn)
            og_ref[r, vs] = (on * _silu(gg_ref[r, vs])).astype(BF)
        return carry

    lax.fori_loop(0, n_chunks, chunk, 0)

    @pl.when(i == pl.num_programs(1) - 1)
    def _():
        sfin_ref[0] = s_scr[...]


def _gla(q, k, v, la, gg, s0, gn, *, B, L, C, tg, s0_per_batch):
    nt = L // tg
    row = lambda b, i: (b * nt + i, 0)
    s0map = (lambda b, i: (b, 0, 0, 0)) if s0_per_batch else (lambda b, i: (0, 0, 0, 0))
    st = (1, GLA_HEADS, GLA_DK, GLA_DV)
    in_specs = [pl.BlockSpec((tg, 256), row), pl.BlockSpec((tg, 256), row), pl.BlockSpec((tg, 512), row),
                pl.BlockSpec((tg, 256), row), pl.BlockSpec((tg, 512), row),
                pl.BlockSpec(st, s0map), pl.BlockSpec((1, GLA_DV), lambda b, i: (0, 0))]
    out_shape = [jax.ShapeDtypeStruct((B * L, 512), BF), jax.ShapeDtypeStruct((B,) + st[1:], F32)]
    out_specs = [pl.BlockSpec((tg, 512), row), pl.BlockSpec(st, lambda b, i: (b, 0, 0, 0))]
    return pl.pallas_call(
        functools.partial(_gla_kernel, C=C, n_chunks=tg // C),
        out_shape=out_shape, grid=(B, nt), in_specs=in_specs, out_specs=out_specs,
        scratch_shapes=[pltpu.VMEM(st[1:], F32)],
        compiler_params=pltpu.CompilerParams(dimension_semantics=("parallel", "arbitrary"),
                                             vmem_limit_bytes=VMEM_LIMIT),
        name="gla_c%d" % C,
    )(q, k, v, la, gg, s0, gn)


ATTN_ROW_BLOCK = 2048


def _lanes(x, n):
    if n % 128 == 0:
        return x if n == 128 else jnp.concatenate([x] * (n // 128), axis=1)
    return x[:, 0:n]


def _attn_kernel(qlat_ref, qr_ref, kcat_ref, *rest, tq, has_meta):
    if has_meta:
        kmeta_ref, o_ref, q_scr, s_scr, v_scr, m_scr, l_scr, acc_scr = rest
    else:
        o_ref, q_scr, s_scr, v_scr, m_scr, l_scr, acc_scr = rest
    i = pl.program_id(1)
    R = MLA_HEADS * tq
    RB = min(R, ATTN_ROW_BLOCK)
    nk = s_scr.shape[2]
    lane_head = lax.broadcasted_iota(jnp.int32, (tq, 256), 1) // MLA_ROPE
    qr = qr_ref[...]
    for hh in range(MLA_HEADS):
        rows = slice(hh * tq, (hh + 1) * tq)
        q_scr[rows, 0:256] = qlat_ref[:, hh * 256:(hh + 1) * 256]
        q_scr[rows, 256:512] = jnp.where(lane_head == hh, qr, jnp.zeros_like(qr))
    m_scr[2] = jnp.full(m_scr.shape[1:], NEG_INF, F32)
    l_scr[...] = jnp.zeros(l_scr.shape, F32)
    acc_scr[...] = jnp.zeros(acc_scr.shape, F32)

    def score(kc, t, mask):
        slot = t % 2
        v_scr[slot] = kc[:, 0:256]
        for rb in range(R // RB):
            rows = slice(rb * RB, (rb + 1) * RB)
            s = _dot_nt(q_scr[rows, :], kc)
            key = lax.broadcasted_iota(jnp.int32, (RB, nk), 1)
            if mask == "causal":
                tok = lax.broadcasted_iota(jnp.int32, (RB, nk), 0) % tq
                s = jnp.where(key <= tok, s, NEG_INF)
            elif mask == "meta":
                s = jnp.where(key < N_META, s, NEG_INF)
            s_scr[slot, rows, :] = s
            m_scr[t % 3, rows, :] = jnp.maximum(m_scr[(t + 2) % 3, rows, :], jnp.max(s, axis=1, keepdims=True))

    def update(t):
        slot = t % 2
        vb = v_scr[slot]
        for rb in range(R // RB):
            rows = slice(rb * RB, (rb + 1) * RB)
            m_new = m_scr[t % 3, rows, :]
            alpha = jnp.exp2(m_scr[(t + 2) % 3, rows, :] - m_new)
            p = jnp.exp2(s_scr[slot, rows, :] - _lanes(m_new, nk))
            psum = p if nk < 128 else sum(p[:, c * 128:(c + 1) * 128] for c in range(nk // 128))
            l_scr[rows, 0:psum.shape[1]] = alpha[:, 0:psum.shape[1]] * l_scr[rows, 0:psum.shape[1]] + psum
            acc_scr[rows, :] = _lanes(alpha, 256) * acc_scr[rows, :] + _dot(p.astype(BF), vb)

    score(kcat_ref[pl.ds(pl.multiple_of(i * tq, tq), tq), :], 0, "causal")
    if has_meta:
        score(kmeta_ref[...], 1, "meta")
        update(0)

        def full(j, carry):
            update(j + 1)
            score(kcat_ref[pl.ds(pl.multiple_of(j * tq, tq), tq), :], j + 2, None)
            return carry

        lax.fori_loop(0, i, full, 0)
        update(i + 1)
    else:
        update(0)
    for hh in range(MLA_HEADS):
        rows = slice(hh * tq, (hh + 1) * tq)
        o = acc_scr[rows, :] * (1.0 / jnp.sum(l_scr[rows, :], axis=1, keepdims=True))
        o_ref[:, hh * 256:(hh + 1) * 256] = o.astype(BF)


def _attn(qlat, qr, kcat, kmeta, *, B, L, tq):
    nq = L // tq
    row = lambda b, i: (b * nq + i, 0)
    in_specs = [pl.BlockSpec((tq, 2048), row), pl.BlockSpec((tq, 256), row),
                pl.BlockSpec((L, 512), lambda b, i: (b, 0))]
    args = [qlat, qr, kcat]
    if kmeta is not None:
        in_specs.append(pl.BlockSpec((tq, 512), lambda b, i: (0, 0)))
        args.append(jnp.pad(kmeta, ((0, tq - N_META), (0, 0))))
    else:
        assert nq == 1
    R = MLA_HEADS * tq
    return pl.pallas_call(
        functools.partial(_attn_kernel, tq=tq, has_meta=kmeta is not None),
        out_shape=jax.ShapeDtypeStruct((B * L, 2048), BF), grid=(B, nq), in_specs=in_specs,
        out_specs=pl.BlockSpec((tq, 2048), row),
        scratch_shapes=[pltpu.VMEM((R, 512), BF), pltpu.VMEM((2, R, tq), F32), pltpu.VMEM((2, tq, 256), BF),
                        pltpu.VMEM((3, R, 128), F32), pltpu.VMEM((R, 128), F32), pltpu.VMEM((R, 256), F32)],
        compiler_params=pltpu.CompilerParams(dimension_semantics=("parallel", "arbitrary"),
                                             vmem_limit_bytes=VMEM_LIMIT),
        name="attn_t%d" % tq,
    )(*args)


NEW_PAD = 16
KEY_CH = 1024


def _pattn_kernel(pt_ref, ql_ref, qr_ref, cn_ref, kn_ref, cckv_hbm, ckr_hbm, o_ref, ckv_buf, kr_buf, kb_scr, s_scr,
                  sem, *, n_pages, t_new):
    b = pl.program_id(0)
    nb = pl.num_programs(0)
    slot = b % 2

    def copies(seq, sl):
        out = []
        for pg in range(n_pages):
            page = pt_ref[seq, pg]
            out.append(pltpu.make_async_copy(cckv_hbm.at[page], ckv_buf.at[sl, pl.ds(pg * PAGE, PAGE), :],
                                             sem.at[0, sl]))
            out.append(pltpu.make_async_copy(ckr_hbm.at[page], kr_buf.at[sl, :, pl.ds(pg * PAGE, PAGE)],
                                             sem.at[1, sl]))
        return out

    @pl.when(b == 0)
    def _():
        for c in copies(0, 0):
            c.start()

    @pl.when(b + 1 < nb)
    def _():
        for c in copies(b + 1, 1 - slot):
            c.start()

    ql = ql_ref[0]
    qr = qr_ref[0]
    R = ql.shape[0]
    cn = cn_ref[0].astype(BF)
    kn = kn_ref[0].astype(BF)
    s_new = _dot_nt(ql, cn) + _dot_nt(qr, kn)
    tok = lax.broadcasted_iota(jnp.int32, (R, NEW_PAD), 0) % t_new
    key = lax.broadcasted_iota(jnp.int32, (R, NEW_PAD), 1)
    s_new = jnp.where(key <= tok, s_new, NEG_INF)

    for c in copies(b, slot):
        c.wait()
    n_keys = n_pages * PAGE
    for ch in range(n_keys // KEY_CH):
        ks = slice(ch * KEY_CH, (ch + 1) * KEY_CH)
        kcb = ckv_buf[slot, ks, :].astype(BF)
        kb_scr[ks, :] = kcb
        s_scr[:, ks] = _dot_nt(ql, kcb) + _dot(qr, kr_buf[slot, :, ks].astype(BF))
    s = s_scr[...]
    m = jnp.maximum(jnp.max(s, axis=1, keepdims=True), jnp.max(s_new, axis=1, keepdims=True))
    p = jnp.exp2(s - m)
    p_new = jnp.exp2(s_new - m)
    l = jnp.sum(p, axis=1, keepdims=True) + jnp.sum(p_new, axis=1, keepdims=True)
    acc = _dot(p.astype(BF), kb_scr[...]) + _dot(p_new.astype(BF), cn)
    o_ref[0] = acc * (1.0 / l)


def _pattn(page_table, ql, qr, cn, kn, cache_ckv, cache_krope_t, *, t_new):
    Bd, n_pages = page_table.shape
    R = ql.shape[1]
    n_keys = n_pages * PAGE
    grid_spec = pltpu.PrefetchScalarGridSpec(
        num_scalar_prefetch=1, grid=(Bd,),
        in_specs=[pl.BlockSpec((1, R, MLA_KV_LORA), lambda b, pt: (b, 0, 0)),
                  pl.BlockSpec((1, R, MLA_ROPE), lambda b, pt: (b, 0, 0)),
                  pl.BlockSpec((1, NEW_PAD, MLA_KV_LORA), lambda b, pt: (b, 0, 0)),
                  pl.BlockSpec((1, NEW_PAD, MLA_ROPE), lambda b, pt: (b, 0, 0)),
                  pl.BlockSpec(memory_space=pl.ANY), pl.BlockSpec(memory_space=pl.ANY)],
        out_specs=pl.BlockSpec((1, R, MLA_KV_LORA), lambda b, pt: (b, 0, 0)),
        scratch_shapes=[pltpu.VMEM((2, n_keys, MLA_KV_LORA), F32), pltpu.VMEM((2, MLA_ROPE, n_keys), F32),
                        pltpu.VMEM((n_keys, MLA_KV_LORA), BF), pltpu.VMEM((R, n_keys), F32),
                        pltpu.SemaphoreType.DMA((2, 2))])
    return pl.pallas_call(
        functools.partial(_pattn_kernel, n_pages=n_pages, t_new=t_new),
        out_shape=jax.ShapeDtypeStruct((Bd, R, MLA_KV_LORA), F32), grid_spec=grid_spec,
        compiler_params=pltpu.CompilerParams(dimension_semantics=("arbitrary",), vmem_limit_bytes=VMEM_LIMIT),
        name="pattn",
    )(page_table, ql, qr, cn, kn, cache_ckv, cache_krope_t)


FF_CH = 256
N_FF_CH = D_FF // FF_CH


def _ffn_kernel(x_ref, og_ref, ol_ref, lng_ref, lnb_ref, wuv_ref, wo_ref, l1g_ref, l1b_ref, wa_ref, wu_ref,
                cw_ref, cb_ref, wd_ref, l2g_ref, l2b_ref, *rest, tm, inject):
    if inject:
        inj1_ref, inj2_ref, m1_ref, m2_ref, y_ref, a_ref = rest
    else:
        halo0_ref, y_ref, alast_ref, halo_scr = rest
        i = pl.program_id(1)

        @pl.when(i == 0)
        def _():
            halo_scr[...] = halo0_ref[...]

    h = _layer_norm(x_ref[...], lng_ref[...], lnb_ref[...])
    om = jnp.concatenate([_dot(ol_ref[:, j * 512:(j + 1) * 512], wuv_ref[j]) for j in range(MLA_HEADS // 2)],
                         axis=1)
    mix = _dot(og_ref[...], wo_ref[0:512, :]) + _dot(om.astype(BF), wo_ref[512:1024, :])
    h1 = _layer_norm(DN_ALPHA * h + mix, l1g_ref[...], l1b_ref[...])
    h1b = h1.astype(BF)
    ridx = lax.broadcasted_iota(jnp.int32, (tm, FF_CH), 0)
    acc = jnp.zeros((tm, D_MODEL), F32)
    for c in range(N_FF_CH):
        cs = slice(c * FF_CH, (c + 1) * FF_CH)
        a = _dot(h1b, wa_ref[:, cs])
        up = _dot(h1b, wu_ref[:, cs])
        r1 = pltpu.roll(a, 1, 0)
        r2 = pltpu.roll(a, 2, 0)
        if inject:
            a1 = jnp.where(m1_ref[...] > 0.0, inj1_ref[:, cs], r1)
            a2 = jnp.where(m2_ref[...] > 0.0, inj2_ref[:, cs], r2)
            a_ref[:, cs] = a
        else:
            hl = halo_scr[:, cs]
            a1 = jnp.where(ridx == 0, hl[1:2, :], r1)
            a2 = jnp.where(ridx == 0, hl[0:1, :], jnp.where(ridx == 1, hl[1:2, :], r2))
            halo_scr[:, cs] = a[tm - 2:tm, :]
            alast_ref[0, :, cs] = a[tm - 2:tm, :]
        conv = cb_ref[:, cs] + cw_ref[0:1, cs] * a2 + cw_ref[1:2, cs] * a1 + cw_ref[2:3, cs] * a
        act = (_silu(conv) * up).astype(BF)
        acc = acc + _dot(act, wd_ref[cs, :])
    y_ref[...] = _layer_norm(DN_ALPHA * h1 + acc, l2g_ref[...], l2b_ref[...])


def _ffn(x, og, ol, w, *, B, L, tm, halo0=None, inject=None):
    nt = L // tm
    T = B * L
    row = lambda b, i: (b * nt + i, 0)
    c2 = lambda b, i: (0, 0)
    c3 = lambda b, i: (0, 0, 0)
    once = pl.Buffered(1)
    in_specs = [pl.BlockSpec((tm, D_MODEL), row), pl.BlockSpec((tm, 512), row), pl.BlockSpec((tm, 2048), row),
                pl.BlockSpec((1, D_MODEL), c2), pl.BlockSpec((1, D_MODEL), c2),
                pl.BlockSpec((MLA_HEADS // 2, 512, 128), c3, pipeline_mode=once),
                pl.BlockSpec((D_MODEL, D_MODEL), c2, pipeline_mode=once),
                pl.BlockSpec((1, D_MODEL), c2), pl.BlockSpec((1, D_MODEL), c2),
                pl.BlockSpec((D_MODEL, D_FF), c2, pipeline_mode=once),
                pl.BlockSpec((D_MODEL, D_FF), c2, pipeline_mode=once),
                pl.BlockSpec((3, D_FF), c2), pl.BlockSpec((1, D_FF), c2),
                pl.BlockSpec((D_FF, D_MODEL), c2, pipeline_mode=once),
                pl.BlockSpec((1, D_MODEL), c2), pl.BlockSpec((1, D_MODEL), c2)]
    args = [x, og, ol, w["lng"], w["lnb"], w["wuv"], w["wo"], w["l1g"], w["l1b"], w["wa"], w["wu"],
            w["cw"], w["cb"], w["wd"], w["l2g"], w["l2b"]]
    scratch = []
    if inject is not None:
        inj1, inj2, m1, m2 = inject
        in_specs += [pl.BlockSpec((tm, D_FF), row), pl.BlockSpec((tm, D_FF), row),
                     pl.BlockSpec((tm, 1), row), pl.BlockSpec((tm, 1), row)]
        args += [inj1, inj2, m1, m2]
        out_shape = [jax.ShapeDtypeStruct((T, D_MODEL), F32), jax.ShapeDtypeStruct((T, D_FF), F32)]
        out_specs = [pl.BlockSpec((tm, D_MODEL), row), pl.BlockSpec((tm, D_FF), row)]
    else:
        in_specs.append(pl.BlockSpec((2, D_FF), c2))
        args.append(halo0)
        out_shape = [jax.ShapeDtypeStruct((T, D_MODEL), F32), jax.ShapeDtypeStruct((B, 2, D_FF), F32)]
        out_specs = [pl.BlockSpec((tm, D_MODEL), row), pl.BlockSpec((1, 2, D_FF), lambda b, i: (b, 0, 0))]
        scratch = [pltpu.VMEM((2, D_FF), F32)]
    return pl.pallas_call(
        functools.partial(_ffn_kernel, tm=tm, inject=inject is not None),
        out_shape=out_shape, grid=(B, nt), in_specs=in_specs, out_specs=out_specs, scratch_shapes=scratch,
        compiler_params=pltpu.CompilerParams(dimension_semantics=("parallel", "arbitrary"),
                                             vmem_limit_bytes=VMEM_LIMIT),
        name="ffn_inject" if inject is not None else "ffn_seq",
    )(*args)


def _rope_tables(pos):
    half = MLA_ROPE // 2
    inv = ROPE_BASE ** (-jnp.arange(half, dtype=F32) / half)
    ang = pos.astype(F32)[:, None] * inv[None, :]
    cos = jnp.concatenate([jnp.cos(ang)] * 2, axis=1)
    sin = jnp.concatenate([jnp.sin(ang)] * 2, axis=1)
    pad = jnp.zeros((pos.shape[0], 128 - MLA_ROPE), F32)
    return (jnp.tile(cos, (1, MLA_HEADS)), jnp.tile(sin, (1, MLA_HEADS)),
            jnp.concatenate([cos, pad], axis=1), jnp.concatenate([sin, pad], axis=1))


def _swap_halves(w):
    half = w.shape[-1] // 2
    return jnp.concatenate([-w[..., half:], w[..., :half]], axis=-1)


def _prep_weights(ln_emb_g, ln_emb_b, w_in, w_gate_up, b_gate, gla_norm_g, mla_q_norm_g, mla_kv_norm_g,
                  w_uq, w_uk, w_uv, w_o, ln1_g, ln1_b, w_ffn_in, conv_w, conv_b, w_down, ln2_g, ln2_b):
    o_alr, o_cq, o_ckv, o_kr = 1536, 1552, 1936, 2192
    w_kr = w_in[:, o_kr:o_kr + MLA_ROPE]
    last = jnp.concatenate([w_kr, _swap_halves(w_kr), w_in[:, o_alr:o_cq],
                            jnp.zeros((D_MODEL, 128 - 2 * MLA_ROPE - GLA_RANK), F32)], axis=1)
    win = jnp.concatenate([w_in[:, :o_alr], w_in[:, o_cq:o_kr], last], axis=1).astype(BF)
    wg = jnp.zeros((128, 256), F32).at[LAST_ALR:LAST_ALR + GLA_RANK].set(w_gate_up).astype(BF)
    uq = w_uq.reshape(MLA_Q_LORA, MLA_HEADS, MLA_NOPE + MLA_ROPE)
    uq_r = uq[:, :, MLA_NOPE:]
    wuq = jnp.concatenate([uq[:, :, :MLA_NOPE].reshape(MLA_Q_LORA, 512), uq_r.reshape(MLA_Q_LORA, 256),
                           _swap_halves(uq_r).reshape(MLA_Q_LORA, 256)], axis=1).astype(BF)
    ukt = jnp.transpose(w_uk.reshape(MLA_KV_LORA, MLA_HEADS, MLA_NOPE), (1, 2, 0))
    z64 = jnp.zeros_like(ukt)
    even = (jnp.arange(MLA_HEADS) % 2 == 0)[:, None, None]
    wuk = jnp.concatenate([jnp.where(even, ukt, z64), jnp.where(even, z64, ukt)], axis=1).astype(BF)
    uv = jnp.transpose(w_uv.reshape(MLA_KV_LORA, MLA_HEADS, MLA_DV), (1, 0, 2))
    uv = uv.reshape(MLA_HEADS // 2, 2, MLA_KV_LORA, MLA_DV)
    zz = jnp.zeros_like(uv[:, 0])
    wuv = jnp.concatenate([jnp.concatenate([uv[:, 0], zz], axis=2), jnp.concatenate([zz, uv[:, 1]], axis=2)],
                          axis=1).astype(BF)
    r = lambda a: a.reshape(1, -1)
    return dict(lng=r(ln_emb_g), lnb=r(ln_emb_b), win=win, wg=wg, bg=r(b_gate), qng=r(mla_q_norm_g),
                kvg=r(mla_kv_norm_g), wuq=wuq, wuk=wuk, wuv=wuv, wo=w_o.astype(BF), l1g=r(ln1_g), l1b=r(ln1_b),
                wa=w_ffn_in[:, :D_FF].astype(BF), wu=w_ffn_in[:, D_FF:].astype(BF), cw=conv_w, cb=r(conv_b),
                wd=w_down.astype(BF), l2g=r(ln2_g), l2b=r(ln2_b), gn=r(gla_norm_g))


def kernel(x_prompt, x_sample, cache_ckv, cache_krope, page_table, state_gla, state_conv, meta_tokens, ln_emb_g, ln_emb_b, w_in, w_gate_up, b_gate, gla_norm_g, mla_q_norm_g, mla_kv_norm_g, w_uq, w_uk, w_uv, w_o, ln1_g, ln1_b, w_ffn_in, conv_w, conv_b, w_down, ln2_g, ln2_b):
    B, S, _ = x_prompt.shape
    Bd, Td, _ = x_sample.shape
    past_len = page_table.shape[1] * PAGE
    n_small = N_META + Bd * Td
    w = _prep_weights(ln_emb_g, ln_emb_b, w_in, w_gate_up, b_gate, gla_norm_g, mla_q_norm_g, mla_kv_norm_g,
                      w_uq, w_uk, w_uv, w_o, ln1_g, ln1_b, w_ffn_in, conv_w, conv_b, w_down, ln2_g, ln2_b)

    xs = jnp.concatenate([meta_tokens.astype(F32), x_sample.reshape(Bd * Td, D_MODEL)], axis=0)
    pos_s = jnp.concatenate([jnp.arange(N_META, dtype=jnp.int32),
                             jnp.tile(past_len + jnp.arange(Td, dtype=jnp.int32), Bd)])
    (qg_s, kg_s, vg_s, la_s, gg_s, qlat_s, qr_s, ckv_s, kr_s, kcat_s) = _proj(
        xs, w, _rope_tables(pos_s), n_small, 1)

    M = N_META
    zero_state = jnp.zeros((1, GLA_HEADS, GLA_DK, GLA_DV), F32)
    og_m, s_meta = _gla(qg_s[:M], kg_s[:M], vg_s[:M], la_s[:M], gg_s[:M], zero_state, w["gn"],
                        B=1, L=M, C=M, tg=M, s0_per_batch=False)
    ol_m = _attn(qlat_s[:M], qr_s[:M], kcat_s[:M], None, B=1, L=M, tq=M)

    TP = 16
    padt = lambda a: jnp.pad(a[M:].reshape(Bd, Td, -1), ((0, 0), (0, TP - Td), (0, 0))).reshape(Bd * TP, -1)
    og_sp, gla_sample = _gla(padt(qg_s), padt(kg_s), padt(vg_s), padt(la_s), padt(gg_s), state_gla, w["gn"],
                             B=Bd, L=TP, C=TP, tg=TP, s0_per_batch=True)
    og_s = og_sp.reshape(Bd, TP, 512)[:, :Td].reshape(Bd * Td, 512)

    heads_first = lambda a, d: jnp.transpose(a[M:].reshape(Bd, Td, MLA_HEADS, d), (0, 2, 1, 3)).reshape(
        Bd, MLA_HEADS * Td, d)
    padn = lambda a: jnp.pad(a[M:].reshape(Bd, Td, -1), ((0, 0), (0, NEW_PAD - Td), (0, 0)))
    o_s = _pattn(page_table, heads_first(qlat_s, MLA_KV_LORA), heads_first(qr_s, MLA_ROPE),
                 padn(ckv_s), padn(kr_s), cache_ckv, jnp.swapaxes(cache_krope, 1, 2), t_new=Td)
    ol_s = jnp.transpose(o_s.reshape(Bd, MLA_HEADS, Td, MLA_KV_LORA), (0, 2, 1, 3)).reshape(
        Bd * Td, MLA_HEADS * MLA_KV_LORA).astype(BF)

    sc = state_conv.astype(F32)
    zrow = jnp.zeros((Bd, 1, D_FF), F32)
    inj1 = jnp.concatenate([sc[:, 1:2], zrow, zrow, zrow], axis=1).reshape(Bd * Td, D_FF)
    inj2 = jnp.concatenate([sc[:, 0:1], sc[:, 1:2], zrow, zrow], axis=1).reshape(Bd * Td, D_FF)
    zmeta = jnp.zeros((M, D_FF), F32)
    t_in = np.concatenate([np.arange(M), np.tile(np.arange(Td), Bd)])
    m1 = jnp.asarray((t_in == 0).astype(np.float32)[:, None])
    m2 = jnp.asarray((t_in <= 1).astype(np.float32)[:, None])
    y_s, a_s = _ffn(xs, jnp.concatenate([og_m, og_s], axis=0), jnp.concatenate([ol_m, ol_s], axis=0), w,
                    B=1, L=n_small, tm=176,
                    inject=(jnp.concatenate([zmeta, inj1], axis=0), jnp.concatenate([zmeta, inj2], axis=0), m1, m2))

    TM = 512
    xp = x_prompt.reshape(B * S, D_MODEL)
    pos_p = N_META + jnp.arange(S, dtype=jnp.int32)
    (qg_p, kg_p, vg_p, la_p, gg_p, qlat_p, qr_p, ckv_p, kr_p, kcat_p) = _proj(
        xp, w, _rope_tables(pos_p), TM, S // TM)
    og_p, gla_p = _gla(qg_p, kg_p, vg_p, la_p, gg_p, s_meta, w["gn"], B=B, L=S, C=GLA_CHUNK, tg=TM,
                       s0_per_batch=False)
    ol_p = _attn(qlat_p, qr_p, kcat_p, kcat_s[:M], B=B, L=S, tq=256)
    y_p, conv_prompt = _ffn(xp, og_p, ol_p, w, B=B, L=S, tm=TM, halo0=a_s[M - 2:M])

    y_prompt = y_p.reshape(B, S, D_MODEL)
    y_sample = y_s[M:].reshape(Bd, Td, D_MODEL)
    ckv_prompt = jnp.concatenate([jnp.broadcast_to(ckv_s[None, :M], (B, M, MLA_KV_LORA)),
                                  ckv_p.reshape(B, S, MLA_KV_LORA)], axis=1)
    krope_prompt = jnp.concatenate([jnp.broadcast_to(kr_s[None, :M], (B, M, MLA_ROPE)),
                                    kr_p.reshape(B, S, MLA_ROPE)], axis=1)
    ckv_sample = ckv_s[M:].reshape(Bd, Td, MLA_KV_LORA)
    krope_sample = kr_s[M:].reshape(Bd, Td, MLA_ROPE)
    conv_sample = a_s[M:].reshape(Bd, Td, D_FF)[:, Td - 2:]
    return (y_prompt, y_sample, ckv_prompt, krope_prompt, gla_p, conv_prompt,
            ckv_sample, krope_sample, gla_sample, conv_sample)
```

```python
import functools

import numpy as np
import jax
import jax.numpy as jnp
from jax import lax
from jax.experimental import pallas as pl
from jax.experimental.pallas import tpu as pltpu

F32 = jnp.float32
BF = jnp.bfloat16

D_MODEL = 1024
N_META = 16
GLA_HEADS = 4
GLA_DK = 64
GLA_DV = 128
GLA_RANK = 16
GLA_TAU = 16.0
GLA_CHUNK = 64
MLA_HEADS = 8
MLA_DV = 64
MLA_NOPE = 64
MLA_ROPE = 32
MLA_Q_LORA = 384
MLA_KV_LORA = 256
MLA_SCALE = (MLA_NOPE + MLA_ROPE) ** -0.5
Q_SCALE = MLA_SCALE * 1.4426950408889634
ROPE_BASE = 10000.0
D_FF = 2816
DN_ALPHA = 2.0 ** 0.25
NORM_EPS = 1e-5
NEG_INF = -1e30
PAGE = 128

C_Q, C_K, C_V, C_G, C_CQ, C_CKV, C_LAST, C_END = 0, 256, 512, 1024, 1536, 1920, 2176, 2304
LAST_ALR = 64

VMEM_LIMIT = 56 * 1024 * 1024


def _dot(a, b):
    return jnp.dot(a, b, preferred_element_type=F32)


def _dot_nt(a, b):
    return lax.dot_general(a, b, (((1,), (1,)), ((), ())), preferred_element_type=F32)


def _dot_tn(a, b):
    return lax.dot_general(a, b, (((0,), (0,)), ((), ())), preferred_element_type=F32)


def _layer_norm(x, g, b):
    mu = jnp.mean(x, axis=-1, keepdims=True)
    xc = x - mu
    var = jnp.mean(xc * xc, axis=-1, keepdims=True)
    return xc * lax.rsqrt(var + NORM_EPS) * g + b


def _rms_norm(x, g):
    ms = jnp.mean(x * x, axis=-1, keepdims=True)
    return x * lax.rsqrt(ms + NORM_EPS) * g


def _silu(x):
    return x * (1.0 / (1.0 + jnp.exp(-x)))


def _proj_kernel(x_ref, lng_ref, lnb_ref, win_ref, wg_ref, bg_ref, qng_ref, kvg_ref, wuq_ref, wuk_ref,
                 cq_ref, sq_ref, ck_ref, sk_ref,
                 qg_ref, kg_ref, vg_ref, la_ref, gg_ref, qlat_ref, qr_ref, ckv_ref, kr_ref, kcat_ref):
    h = _layer_norm(x_ref[...], lng_ref[...], lnb_ref[...])
    z = _dot(h.astype(BF), win_ref[...])
    qg_ref[...] = z[:, C_Q:C_K] * (GLA_DK ** -0.5)
    kg_ref[...] = z[:, C_K:C_V]
    vg_ref[...] = z[:, C_V:C_G].astype(BF)
    gg_ref[...] = z[:, C_G:C_CQ]
    last = z[:, C_LAST:C_END]
    y = _dot(last.astype(BF), wg_ref[...]) + bg_ref[...]
    la_ref[...] = (jnp.minimum(y, 0.0) - jnp.log1p(jnp.exp(-jnp.abs(y)))) * (1.0 / GLA_TAU)
    cqn = _rms_norm(z[:, C_CQ:C_CKV], qng_ref[...])
    qa = _dot(cqn.astype(BF), wuq_ref[...])
    qn = (qa[:, 0:512] * Q_SCALE).astype(BF)
    for hh in range(MLA_HEADS):
        pair = hh // 2
        qlat_ref[:, hh * 256:(hh + 1) * 256] = _dot(qn[:, pair * 128:(pair + 1) * 128], wuk_ref[hh]).astype(BF)
    qr = qa[:, 512:768] * cq_ref[...] + qa[:, 768:1024] * sq_ref[...]
    qr_ref[...] = (qr * Q_SCALE).astype(BF)
    ckv = _rms_norm(z[:, C_CKV:C_LAST], kvg_ref[...])
    ckv_ref[...] = ckv
    kr = last * ck_ref[...] + pltpu.roll(last, 96, 1) * sk_ref[...]
    kr_ref[...] = kr[:, 0:MLA_ROPE]
    kr2 = kr + pltpu.roll(kr, 32, 1)
    kr4 = kr2 + pltpu.roll(kr2, 64, 1)
    kcat_ref[:, 0:256] = ckv.astype(BF)
    kcat_ref[:, 256:384] = kr4.astype(BF)
    kcat_ref[:, 384:512] = kr4.astype(BF)


def _proj(x, w, tabs, tm, tab_blocks):
    T = x.shape[0]
    nt = T // tm
    cq, sq, ck, sk = tabs
    row = lambda i: (i, 0)
    tab = (lambda i: (i % tab_blocks, 0)) if tab_blocks > 1 else (lambda i: (0, 0))
    c2 = lambda i: (0, 0)
    c3 = lambda i: (0, 0, 0)
    in_specs = [
        pl.BlockSpec((tm, D_MODEL), row),
        pl.BlockSpec((1, D_MODEL), c2), pl.BlockSpec((1, D_MODEL), c2),
        pl.BlockSpec((D_MODEL, C_END), c2),
        pl.BlockSpec((128, 256), c2), pl.BlockSpec((1, 256), c2),
        pl.BlockSpec((1, MLA_Q_LORA), c2), pl.BlockSpec((1, MLA_KV_LORA), c2),
        pl.BlockSpec((MLA_Q_LORA, 1024), c2),
        pl.BlockSpec((MLA_HEADS, 128, 256), c3),
        pl.BlockSpec((tm, 256), tab), pl.BlockSpec((tm, 256), tab),
        pl.BlockSpec((tm, 128), tab), pl.BlockSpec((tm, 128), tab),
    ]
    outs = [(256, F32), (256, F32), (512, BF), (256, F32), (512, F32), (2048, BF), (256, BF), (256, F32),
            (MLA_ROPE, F32), (512, BF)]
    out_shape = [jax.ShapeDtypeStruct((T, wd), dt) for wd, dt in outs]
    out_specs = [pl.BlockSpec((tm, wd), row) for wd, _ in outs]
    return pl.pallas_call(
        _proj_kernel, out_shape=out_shape, grid=(nt,), in_specs=in_specs, out_specs=out_specs,
        compiler_params=pltpu.CompilerParams(dimension_semantics=("parallel",), vmem_limit_bytes=VMEM_LIMIT),
        name="proj",
    )(x, w["lng"], w["lnb"], w["win"], w["wg"], w["bg"], w["qng"], w["kvg"], w["wuq"], w["wuk"], cq, sq, ck, sk)


GLA_SEQ_PER_STEP = 8

def _gla_kernel(q_ref, k_ref, v_ref, la_ref, gg_ref, s0_ref, gn_ref, og_ref, sfin_ref, s_scr, *, C, n_chunks,
                independent):
    if not independent:
        i = pl.program_id(1)

        @pl.when(i == 0)
        def _():
            s_scr[...] = s0_ref[0]

        state = [s_scr[hh] for hh in range(GLA_HEADS)]

    row = lax.broadcasted_iota(jnp.int32, (C, C), 0)
    col = lax.broadcasted_iota(jnp.int32, (C, C), 1)
    tril = col <= row
    tri_b = jnp.where(tril, 1.0, 0.0).astype(BF)
    eye = (lax.broadcasted_iota(jnp.int32, (128, 128), 0) == lax.broadcasted_iota(jnp.int32, (128, 128), 1))
    lo_half = lax.broadcasted_iota(jnp.int32, (C, 128), 1) < GLA_DK
    hi_half = jnp.logical_not(lo_half)
    mid = C // 2
    gn = gn_ref[...]

    for c in range(n_chunks):
        r = slice(c * C, (c + 1) * C)
        if independent:
            state = [s0_ref[c, hh] for hh in range(GLA_HEADS)]
        g = la_ref[r, :]
        g1 = g.astype(BF)
        r1 = g - g1.astype(F32)
        g2 = r1.astype(BF)
        g3 = (r1 - g2.astype(F32)).astype(BF)
        b = _dot(tri_b, g1) + _dot(tri_b, g2) + _dot(tri_b, g3)
        b_mid = b[mid:mid + 1, :]
        b_last = b[C - 1:C, :]
        q = q_ref[r, :]
        k = k_ref[r, :]
        qe = (q * jnp.exp(b)).astype(BF)
        qm = (q * jnp.exp(b - b_mid)).astype(BF)
        km = (k * jnp.exp(b_mid - b)).astype(BF)
        kl = k * jnp.exp(b_last - b)
        dec = jnp.exp(b_last)
        for pr in range(GLA_HEADS // 2):
            pls = slice(pr * 128, (pr + 1) * 128)
            km_p = km[:, pls]
            klt = jnp.transpose(kl[:, pls]).astype(BF)
            dcol = jnp.sum(jnp.where(eye, jnp.broadcast_to(dec[:, pls], (128, 128)), 0.0),
                           axis=1, keepdims=True)
            s_pair = jnp.concatenate([state[2 * pr], state[2 * pr + 1]], axis=0).astype(BF)
            for jj in range(2):
                hh = 2 * pr + jj
                vs = slice(hh * GLA_DV, (hh + 1) * GLA_DV)
                rs = slice(jj * GLA_DK, (jj + 1) * GLA_DK)
                mine = lo_half if jj == 0 else hi_half
                v = v_ref[r, vs]
                o = _dot(jnp.where(mine, qe[:, pls], jnp.zeros((C, 128), BF)), s_pair)
                att = _dot_nt(jnp.where(mine, qm[:, pls], jnp.zeros((C, 128), BF)), km_p)
                o = o + _dot(jnp.where(tril, att, 0.0).astype(BF), v)
                state[hh] = state[hh] * dcol[rs, :] + _dot(klt[rs, :], v)
                if independent:
                    sfin_ref[c, hh] = state[hh]
                on = _rms_norm(o, gn)
                og_ref[r, vs] = (on * _silu(gg_ref[r, vs])).astype(BF)

    if not independent:
        for hh in range(GLA_HEADS):
            s_scr[hh] = state[hh]

        @pl.when(i == pl.num_programs(1) - 1)
        def _():
            for hh in range(GLA_HEADS):
                sfin_ref[0, hh] = state[hh]


def _gla(q, k, v, la, gg, s0, gn, *, B, L, C, tg, independent=False):
    nt = L // tg
    n_chunks = tg // C
    row = lambda b, i: (b * nt + i, 0)
    st = (GLA_HEADS, GLA_DK, GLA_DV)
    if independent:
        assert nt == 1
        s0_spec = pl.BlockSpec((n_chunks,) + st, lambda b, i: (b, 0, 0, 0))
        sfin_spec = s0_spec
        n_states = B * n_chunks
    else:
        s0_spec = pl.BlockSpec((1,) + st, lambda b, i: (0, 0, 0, 0))
        sfin_spec = pl.BlockSpec((1,) + st, lambda b, i: (b, 0, 0, 0))
        n_states = B
    in_specs = [pl.BlockSpec((tg, 256), row), pl.BlockSpec((tg, 256), row), pl.BlockSpec((tg, 512), row),
                pl.BlockSpec((tg, 256), row), pl.BlockSpec((tg, 512), row),
                s0_spec, pl.BlockSpec((1, GLA_DV), lambda b, i: (0, 0))]
    out_shape = [jax.ShapeDtypeStruct((B * L, 512), BF), jax.ShapeDtypeStruct((n_states,) + st, F32)]
    out_specs = [pl.BlockSpec((tg, 512), row), sfin_spec]
    return pl.pallas_call(
        functools.partial(_gla_kernel, C=C, n_chunks=n_chunks, independent=independent),
        out_shape=out_shape, grid=(B, nt), in_specs=in_specs, out_specs=out_specs,
        scratch_shapes=[pltpu.VMEM(st, F32)],
        compiler_params=pltpu.CompilerParams(dimension_semantics=("parallel", "arbitrary"),
                                             vmem_limit_bytes=VMEM_LIMIT),
        name="gla_c%d%s" % (C, "_ind" if independent else ""),
    )(q, k, v, la, gg, s0, gn)


ATTN_ROW_BLOCK = 2048


def _lanes(x, n):
    if n % 128 == 0:
        return x if n == 128 else jnp.concatenate([x] * (n // 128), axis=1)
    return x[:, 0:n]


def _attn_kernel(qlat_ref, qr_ref, kcat_ref, *rest, tq, has_meta):
    if has_meta:
        kmeta_ref, o_ref, q_scr, s_scr, v_scr, m_scr, l_scr, acc_scr = rest
    else:
        o_ref, q_scr, s_scr, v_scr, m_scr, l_scr, acc_scr = rest
    i = pl.program_id(1)
    R = MLA_HEADS * tq
    RB = min(R, ATTN_ROW_BLOCK)
    nk = s_scr.shape[2]
    lane_head = lax.broadcasted_iota(jnp.int32, (tq, 256), 1) // MLA_ROPE
    qr = qr_ref[...]
    for hh in range(MLA_HEADS):
        rows = slice(hh * tq, (hh + 1) * tq)
        q_scr[rows, 0:256] = qlat_ref[:, hh * 256:(hh + 1) * 256]
        q_scr[rows, 256:512] = jnp.where(lane_head == hh, qr, jnp.zeros_like(qr))
    m_scr[2] = jnp.full(m_scr.shape[1:], NEG_INF, F32)
    l_scr[...] = jnp.zeros(l_scr.shape, F32)
    acc_scr[...] = jnp.zeros(acc_scr.shape, F32)

    def score(kc, t, mask):
        slot = t % 2
        v_scr[slot] = kc[:, 0:256]
        for rb in range(R // RB):
            rows = slice(rb * RB, (rb + 1) * RB)
            s = _dot_nt(q_scr[rows, :], kc)
            key = lax.broadcasted_iota(jnp.int32, (RB, nk), 1)
            if mask == "causal":
                tok = lax.broadcasted_iota(jnp.int32, (RB, nk), 0) % tq
                s = jnp.where(key <= tok, s, NEG_INF)
            elif mask == "meta":
                s = jnp.where(key < N_META, s, NEG_INF)
            s_scr[slot, rows, :] = s
            m_scr[t % 3, rows, :] = jnp.maximum(m_scr[(t + 2) % 3, rows, :], jnp.max(s, axis=1, keepdims=True))

    def update(t):
        slot = t % 2
        vb = v_scr[slot]
        for rb in range(R // RB):
            rows = slice(rb * RB, (rb + 1) * RB)
            m_new = m_scr[t % 3, rows, :]
            alpha = jnp.exp2(m_scr[(t + 2) % 3, rows, :] - m_new)
            p = jnp.exp2(s_scr[slot, rows, :] - _lanes(m_new, nk))
            psum = p if nk < 128 else sum(p[:, c * 128:(c + 1) * 128] for c in range(nk // 128))
            l_scr[rows, 0:psum.shape[1]] = alpha[:, 0:psum.shape[1]] * l_scr[rows, 0:psum.shape[1]] + psum
            acc_scr[rows, :] = _lanes(alpha, 256) * acc_scr[rows, :] + _dot(p.astype(BF), vb)

    score(kcat_ref[pl.ds(pl.multiple_of(i * tq, tq), tq), :], 0, "causal")
    if has_meta:
        score(kmeta_ref[...], 1, "meta")
        update(0)

        def full(j, carry):
            update(j + 1)
            score(kcat_ref[pl.ds(pl.multiple_of(j * tq, tq), tq), :], j + 2, None)
            return carry

        lax.fori_loop(0, i, full, 0)
        update(i + 1)
    else:
        update(0)
    for hh in range(MLA_HEADS):
        rows = slice(hh * tq, (hh + 1) * tq)
        o = acc_scr[rows, :] * (1.0 / jnp.sum(l_scr[rows, :], axis=1, keepdims=True))
        o_ref[:, hh * 256:(hh + 1) * 256] = o.astype(BF)


def _attn(qlat, qr, kcat, kmeta, *, B, L, tq):
    nq = L // tq
    row = lambda b, i: (b * nq + i, 0)
    in_specs = [pl.BlockSpec((tq, 2048), row), pl.BlockSpec((tq, 256), row),
                pl.BlockSpec((L, 512), lambda b, i: (b, 0))]
    args = [qlat, qr, kcat]
    if kmeta is not None:
        in_specs.append(pl.BlockSpec((tq, 512), lambda b, i: (0, 0)))
        args.append(jnp.pad(kmeta, ((0, tq - N_META), (0, 0))))
    else:
        assert nq == 1
    R = MLA_HEADS * tq
    return pl.pallas_call(
        functools.partial(_attn_kernel, tq=tq, has_meta=kmeta is not None),
        out_shape=jax.ShapeDtypeStruct((B * L, 2048), BF), grid=(B, nq), in_specs=in_specs,
        out_specs=pl.BlockSpec((tq, 2048), row),
        scratch_shapes=[pltpu.VMEM((R, 512), BF), pltpu.VMEM((2, R, tq), F32), pltpu.VMEM((2, tq, 256), BF),
                        pltpu.VMEM((3, R, 128), F32), pltpu.VMEM((R, 128), F32), pltpu.VMEM((R, 256), F32)],
        compiler_params=pltpu.CompilerParams(dimension_semantics=("parallel", "arbitrary"),
                                             vmem_limit_bytes=VMEM_LIMIT),
        name="attn_t%d" % tq,
    )(*args)


NEW_PAD = 16
KEY_CH = 1024


def _pattn_kernel(pt_ref, ql_ref, qr_ref, cn_ref, kn_ref, cckv_hbm, ckr_hbm, o_ref, ckv_buf, kr_buf, kb_scr, s_scr,
                  sem, *, n_pages, t_new):
    b = pl.program_id(0)
    nb = pl.num_programs(0)
    slot = b % 2

    def copies(seq, sl):
        out = []
        for pg in range(n_pages):
            page = pt_ref[seq, pg]
            out.append(pltpu.make_async_copy(cckv_hbm.at[page], ckv_buf.at[sl, pl.ds(pg * PAGE, PAGE), :],
                                             sem.at[0, sl]))
            out.append(pltpu.make_async_copy(ckr_hbm.at[page], kr_buf.at[sl, :, pl.ds(pg * PAGE, PAGE)],
                                             sem.at[1, sl]))
        return out

    @pl.when(b == 0)
    def _():
        for c in copies(0, 0):
            c.start()

    @pl.when(b + 1 < nb)
    def _():
        for c in copies(b + 1, 1 - slot):
            c.start()

    ql = ql_ref[0]
    qr = qr_ref[0]
    R = ql.shape[0]
    cn = cn_ref[0].astype(BF)
    kn = kn_ref[0].astype(BF)
    s_new = _dot_nt(ql, cn) + _dot_nt(qr, kn)
    tok = lax.broadcasted_iota(jnp.int32, (R, NEW_PAD), 0) % t_new
    key = lax.broadcasted_iota(jnp.int32, (R, NEW_PAD), 1)
    s_new = jnp.where(key <= tok, s_new, NEG_INF)

    for c in copies(b, slot):
        c.wait()
    n_keys = n_pages * PAGE
    for ch in range(n_keys // KEY_CH):
        ks = slice(ch * KEY_CH, (ch + 1) * KEY_CH)
        kcb = ckv_buf[slot, ks, :].astype(BF)
        kb_scr[ks, :] = kcb
        s_scr[:, ks] = _dot_nt(ql, kcb) + _dot(qr, kr_buf[slot, :, ks].astype(BF))
    s = s_scr[...]
    m = jnp.maximum(jnp.max(s, axis=1, keepdims=True), jnp.max(s_new, axis=1, keepdims=True))
    p = jnp.exp2(s - m)
    p_new = jnp.exp2(s_new - m)
    l = jnp.sum(p, axis=1, keepdims=True) + jnp.sum(p_new, axis=1, keepdims=True)
    acc = _dot(p.astype(BF), kb_scr[...]) + _dot(p_new.astype(BF), cn)
    o_ref[0] = acc * (1.0 / l)


def _pattn(page_table, ql, qr, cn, kn, cache_ckv, cache_krope_t, *, t_new):
    Bd, n_pages = page_table.shape
    R = ql.shape[1]
    n_keys = n_pages * PAGE
    grid_spec = pltpu.PrefetchScalarGridSpec(
        num_scalar_prefetch=1, grid=(Bd,),
        in_specs=[pl.BlockSpec((1, R, MLA_KV_LORA), lambda b, pt: (b, 0, 0)),
                  pl.BlockSpec((1, R, MLA_ROPE), lambda b, pt: (b, 0, 0)),
                  pl.BlockSpec((1, NEW_PAD, MLA_KV_LORA), lambda b, pt: (b, 0, 0)),
                  pl.BlockSpec((1, NEW_PAD, MLA_ROPE), lambda b, pt: (b, 0, 0)),
                  pl.BlockSpec(memory_space=pl.ANY), pl.BlockSpec(memory_space=pl.ANY)],
        out_specs=pl.BlockSpec((1, R, MLA_KV_LORA), lambda b, pt: (b, 0, 0)),
        scratch_shapes=[pltpu.VMEM((2, n_keys, MLA_KV_LORA), F32), pltpu.VMEM((2, MLA_ROPE, n_keys), F32),
                        pltpu.VMEM((n_keys, MLA_KV_LORA), BF), pltpu.VMEM((R, n_keys), F32),
                        pltpu.SemaphoreType.DMA((2, 2))])
    return pl.pallas_call(
        functools.partial(_pattn_kernel, n_pages=n_pages, t_new=t_new),
        out_shape=jax.ShapeDtypeStruct((Bd, R, MLA_KV_LORA), F32), grid_spec=grid_spec,
        compiler_params=pltpu.CompilerParams(dimension_semantics=("arbitrary",), vmem_limit_bytes=VMEM_LIMIT),
        name="pattn",
    )(page_table, ql, qr, cn, kn, cache_ckv, cache_krope_t)


FF_CH = 256
N_FF_CH = D_FF // FF_CH


def _ffn_kernel(x_ref, og_ref, ol_ref, lng_ref, lnb_ref, wuv_ref, wo_ref, l1g_ref, l1b_ref, wa_ref, wu_ref,
                cw_ref, cb_ref, wd_ref, l2g_ref, l2b_ref, *rest, tm, inject):
    if inject:
        inj1_ref, inj2_ref, m1_ref, m2_ref, y_ref, a_ref, act_scr = rest
    else:
        halo0_ref, y_ref, alast_ref, act_scr, halo_scr = rest
        i = pl.program_id(1)

        @pl.when(i == 0)
        def _():
            halo_scr[...] = halo0_ref[...]

    h = _layer_norm(x_ref[...], lng_ref[...], lnb_ref[...])
    om = jnp.concatenate([_dot(ol_ref[:, j * 512:(j + 1) * 512], wuv_ref[j]) for j in range(MLA_HEADS // 2)],
                         axis=1)
    mix = _dot(og_ref[...], wo_ref[0:512, :]) + _dot(om.astype(BF), wo_ref[512:1024, :])
    h1 = _layer_norm(DN_ALPHA * h + mix, l1g_ref[...], l1b_ref[...])
    h1b = h1.astype(BF)
    ridx = lax.broadcasted_iota(jnp.int32, (tm, FF_CH), 0)
    for c in range(N_FF_CH):
        cs = slice(c * FF_CH, (c + 1) * FF_CH)
        a = _dot(h1b, wa_ref[:, cs])
        up = _dot(h1b, wu_ref[:, cs])
        r1 = pltpu.roll(a, 1, 0)
        r2 = pltpu.roll(a, 2, 0)
        if inject:
            a1 = jnp.where(m1_ref[...] > 0.0, inj1_ref[:, cs], r1)
            a2 = jnp.where(m2_ref[...] > 0.0, inj2_ref[:, cs], r2)
            a_ref[:, cs] = a
        else:
            hl = halo_scr[:, cs]
            a1 = jnp.where(ridx == 0, hl[1:2, :], r1)
            a2 = jnp.where(ridx == 0, hl[0:1, :], jnp.where(ridx == 1, hl[1:2, :], r2))
            halo_scr[:, cs] = a[tm - 2:tm, :]
            alast_ref[0, :, cs] = a[tm - 2:tm, :]
        conv = cb_ref[:, cs] + cw_ref[0:1, cs] * a2 + cw_ref[1:2, cs] * a1 + cw_ref[2:3, cs] * a
        act_scr[:, cs] = (_silu(conv) * up).astype(BF)
    ffn = _dot(act_scr[...], wd_ref[...])
    y_ref[...] = _layer_norm(DN_ALPHA * h1 + ffn, l2g_ref[...], l2b_ref[...])


def _ffn(x, og, ol, w, *, B, L, tm, halo0=None, inject=None):
    nt = L // tm
    T = B * L
    row = lambda b, i: (b * nt + i, 0)
    c2 = lambda b, i: (0, 0)
    c3 = lambda b, i: (0, 0, 0)
    once = pl.Buffered(1)
    in_specs = [pl.BlockSpec((tm, D_MODEL), row), pl.BlockSpec((tm, 512), row), pl.BlockSpec((tm, 2048), row),
                pl.BlockSpec((1, D_MODEL), c2), pl.BlockSpec((1, D_MODEL), c2),
                pl.BlockSpec((MLA_HEADS // 2, 512, 128), c3, pipeline_mode=once),
                pl.BlockSpec((D_MODEL, D_MODEL), c2, pipeline_mode=once),
                pl.BlockSpec((1, D_MODEL), c2), pl.BlockSpec((1, D_MODEL), c2),
                pl.BlockSpec((D_MODEL, D_FF), c2, pipeline_mode=once),
                pl.BlockSpec((D_MODEL, D_FF), c2, pipeline_mode=once),
                pl.BlockSpec((3, D_FF), c2), pl.BlockSpec((1, D_FF), c2),
                pl.BlockSpec((D_FF, D_MODEL), c2, pipeline_mode=once),
                pl.BlockSpec((1, D_MODEL), c2), pl.BlockSpec((1, D_MODEL), c2)]
    args = [x, og, ol, w["lng"], w["lnb"], w["wuv"], w["wo"], w["l1g"], w["l1b"], w["wa"], w["wu"],
            w["cw"], w["cb"], w["wd"], w["l2g"], w["l2b"]]
    scratch = [pltpu.VMEM((tm, D_FF), BF)]
    if inject is not None:
        inj1, inj2, m1, m2 = inject
        in_specs += [pl.BlockSpec((tm, D_FF), row), pl.BlockSpec((tm, D_FF), row),
                     pl.BlockSpec((tm, 1), row), pl.BlockSpec((tm, 1), row)]
        args += [inj1, inj2, m1, m2]
        out_shape = [jax.ShapeDtypeStruct((T, D_MODEL), F32), jax.ShapeDtypeStruct((T, D_FF), F32)]
        out_specs = [pl.BlockSpec((tm, D_MODEL), row), pl.BlockSpec((tm, D_FF), row)]
    else:
        in_specs.append(pl.BlockSpec((2, D_FF), c2))
        args.append(halo0)
        out_shape = [jax.ShapeDtypeStruct((T, D_MODEL), F32), jax.ShapeDtypeStruct((B, 2, D_FF), F32)]
        out_specs = [pl.BlockSpec((tm, D_MODEL), row), pl.BlockSpec((1, 2, D_FF), lambda b, i: (b, 0, 0))]
        scratch.append(pltpu.VMEM((2, D_FF), F32))
    return pl.pallas_call(
        functools.partial(_ffn_kernel, tm=tm, inject=inject is not None),
        out_shape=out_shape, grid=(B, nt), in_specs=in_specs, out_specs=out_specs, scratch_shapes=scratch,
        compiler_params=pltpu.CompilerParams(dimension_semantics=("parallel", "arbitrary"),
                                             vmem_limit_bytes=VMEM_LIMIT),
        name="ffn_inject" if inject is not None else "ffn_seq",
    )(*args)


def _rope_tables(pos):
    half = MLA_ROPE // 2
    inv = ROPE_BASE ** (-jnp.arange(half, dtype=F32) / half)
    ang = pos.astype(F32)[:, None] * inv[None, :]
    cos = jnp.concatenate([jnp.cos(ang)] * 2, axis=1)
    sin = jnp.concatenate([jnp.sin(ang)] * 2, axis=1)
    pad = jnp.zeros((pos.shape[0], 128 - MLA_ROPE), F32)
    return (jnp.tile(cos, (1, MLA_HEADS)), jnp.tile(sin, (1, MLA_HEADS)),
            jnp.concatenate([cos, pad], axis=1), jnp.concatenate([sin, pad], axis=1))


def _swap_halves(w):
    half = w.shape[-1] // 2
    return jnp.concatenate([-w[..., half:], w[..., :half]], axis=-1)


def _prep_weights(ln_emb_g, ln_emb_b, w_in, w_gate_up, b_gate, gla_norm_g, mla_q_norm_g, mla_kv_norm_g,
                  w_uq, w_uk, w_uv, w_o, ln1_g, ln1_b, w_ffn_in, conv_w, conv_b, w_down, ln2_g, ln2_b):
    o_alr, o_cq, o_ckv, o_kr = 1536, 1552, 1936, 2192
    w_kr = w_in[:, o_kr:o_kr + MLA_ROPE]
    last = jnp.concatenate([w_kr, _swap_halves(w_kr), w_in[:, o_alr:o_cq],
                            jnp.zeros((D_MODEL, 128 - 2 * MLA_ROPE - GLA_RANK), F32)], axis=1)
    win = jnp.concatenate([w_in[:, :o_alr], w_in[:, o_cq:o_kr], last], axis=1).astype(BF)
    wg = jnp.zeros((128, 256), F32).at[LAST_ALR:LAST_ALR + GLA_RANK].set(w_gate_up).astype(BF)
    uq = w_uq.reshape(MLA_Q_LORA, MLA_HEADS, MLA_NOPE + MLA_ROPE)
    uq_r = uq[:, :, MLA_NOPE:]
    wuq = jnp.concatenate([uq[:, :, :MLA_NOPE].reshape(MLA_Q_LORA, 512), uq_r.reshape(MLA_Q_LORA, 256),
                           _swap_halves(uq_r).reshape(MLA_Q_LORA, 256)], axis=1).astype(BF)
    ukt = jnp.transpose(w_uk.reshape(MLA_KV_LORA, MLA_HEADS, MLA_NOPE), (1, 2, 0))
    z64 = jnp.zeros_like(ukt)
    even = (jnp.arange(MLA_HEADS) % 2 == 0)[:, None, None]
    wuk = jnp.concatenate([jnp.where(even, ukt, z64), jnp.where(even, z64, ukt)], axis=1).astype(BF)
    uv = jnp.transpose(w_uv.reshape(MLA_KV_LORA, MLA_HEADS, MLA_DV), (1, 0, 2))
    uv = uv.reshape(MLA_HEADS // 2, 2, MLA_KV_LORA, MLA_DV)
    zz = jnp.zeros_like(uv[:, 0])
    wuv = jnp.concatenate([jnp.concatenate([uv[:, 0], zz], axis=2), jnp.concatenate([zz, uv[:, 1]], axis=2)],
                          axis=1).astype(BF)
    r = lambda a: a.reshape(1, -1)
    return dict(lng=r(ln_emb_g), lnb=r(ln_emb_b), win=win, wg=wg, bg=r(b_gate), qng=r(mla_q_norm_g),
                kvg=r(mla_kv_norm_g), wuq=wuq, wuk=wuk, wuv=wuv, wo=w_o.astype(BF), l1g=r(ln1_g), l1b=r(ln1_b),
                wa=w_ffn_in[:, :D_FF].astype(BF), wu=w_ffn_in[:, D_FF:].astype(BF), cw=conv_w, cb=r(conv_b),
                wd=w_down.astype(BF), l2g=r(ln2_g), l2b=r(ln2_b), gn=r(gla_norm_g))


def kernel(x_prompt, x_sample, cache_ckv, cache_krope, page_table, state_gla, state_conv, meta_tokens, ln_emb_g, ln_emb_b, w_in, w_gate_up, b_gate, gla_norm_g, mla_q_norm_g, mla_kv_norm_g, w_uq, w_uk, w_uv, w_o, ln1_g, ln1_b, w_ffn_in, conv_w, conv_b, w_down, ln2_g, ln2_b):
    B, S, _ = x_prompt.shape
    Bd, Td, _ = x_sample.shape
    past_len = page_table.shape[1] * PAGE
    n_small = N_META + Bd * Td
    w = _prep_weights(ln_emb_g, ln_emb_b, w_in, w_gate_up, b_gate, gla_norm_g, mla_q_norm_g, mla_kv_norm_g,
                      w_uq, w_uk, w_uv, w_o, ln1_g, ln1_b, w_ffn_in, conv_w, conv_b, w_down, ln2_g, ln2_b)

    xs = jnp.concatenate([meta_tokens.astype(F32), x_sample.reshape(Bd * Td, D_MODEL)], axis=0)
    pos_s = jnp.concatenate([jnp.arange(N_META, dtype=jnp.int32),
                             jnp.tile(past_len + jnp.arange(Td, dtype=jnp.int32), Bd)])
    (qg_s, kg_s, vg_s, la_s, gg_s, qlat_s, qr_s, ckv_s, kr_s, kcat_s) = _proj(
        xs, w, _rope_tables(pos_s), n_small, 1)

    M = N_META
    zero_state = jnp.zeros((1, GLA_HEADS, GLA_DK, GLA_DV), F32)
    og_m, s_meta = _gla(qg_s[:M], kg_s[:M], vg_s[:M], la_s[:M], gg_s[:M], zero_state, w["gn"],
                        B=1, L=M, C=M, tg=M)
    ol_m = _attn(qlat_s[:M], qr_s[:M], kcat_s[:M], None, B=1, L=M, tq=M)

    TP = 16
    padt = lambda a: jnp.pad(a[M:].reshape(Bd, Td, -1), ((0, 0), (0, TP - Td), (0, 0))).reshape(Bd * TP, -1)
    og_sp, gla_sample = _gla(padt(qg_s), padt(kg_s), padt(vg_s), padt(la_s), padt(gg_s), state_gla, w["gn"],
                             B=Bd // GLA_SEQ_PER_STEP, L=GLA_SEQ_PER_STEP * TP, C=TP, tg=GLA_SEQ_PER_STEP * TP,
                             independent=True)
    og_s = og_sp.reshape(Bd, TP, 512)[:, :Td].reshape(Bd * Td, 512)

    heads_first = lambda a, d: jnp.transpose(a[M:].reshape(Bd, Td, MLA_HEADS, d), (0, 2, 1, 3)).reshape(
        Bd, MLA_HEADS * Td, d)
    padn = lambda a: jnp.pad(a[M:].reshape(Bd, Td, -1), ((0, 0), (0, NEW_PAD - Td), (0, 0)))
    o_s = _pattn(page_table, heads_first(qlat_s, MLA_KV_LORA), heads_first(qr_s, MLA_ROPE),
                 padn(ckv_s), padn(kr_s), cache_ckv, jnp.swapaxes(cache_krope, 1, 2), t_new=Td)
    ol_s = jnp.transpose(o_s.reshape(Bd, MLA_HEADS, Td, MLA_KV_LORA), (0, 2, 1, 3)).reshape(
        Bd * Td, MLA_HEADS * MLA_KV_LORA).astype(BF)

    sc = state_conv.astype(F32)
    zrow = jnp.zeros((Bd, 1, D_FF), F32)
    inj1 = jnp.concatenate([sc[:, 1:2], zrow, zrow, zrow], axis=1).reshape(Bd * Td, D_FF)
    inj2 = jnp.concatenate([sc[:, 0:1], sc[:, 1:2], zrow, zrow], axis=1).reshape(Bd * Td, D_FF)
    zmeta = jnp.zeros((M, D_FF), F32)
    t_in = np.concatenate([np.arange(M), np.tile(np.arange(Td), Bd)])
    m1 = jnp.asarray((t_in == 0).astype(np.float32)[:, None])
    m2 = jnp.asarray((t_in <= 1).astype(np.float32)[:, None])
    y_s, a_s = _ffn(xs, jnp.concatenate([og_m, og_s], axis=0), jnp.concatenate([ol_m, ol_s], axis=0), w,
                    B=1, L=n_small, tm=176,
                    inject=(jnp.concatenate([zmeta, inj1], axis=0), jnp.concatenate([zmeta, inj2], axis=0), m1, m2))

    TM = 512
    xp = x_prompt.reshape(B * S, D_MODEL)
    pos_p = N_META + jnp.arange(S, dtype=jnp.int32)
    (qg_p, kg_p, vg_p, la_p, gg_p, qlat_p, qr_p, ckv_p, kr_p, kcat_p) = _proj(
        xp, w, _rope_tables(pos_p), TM, S // TM)
    og_p, gla_p = _gla(qg_p, kg_p, vg_p, la_p, gg_p, s_meta, w["gn"], B=B, L=S, C=GLA_CHUNK, tg=TM)
    ol_p = _attn(qlat_p, qr_p, kcat_p, kcat_s[:M], B=B, L=S, tq=256)
    y_p, conv_prompt = _ffn(xp, og_p, ol_p, w, B=B, L=S, tm=TM, halo0=a_s[M - 2:M])

    y_prompt = y_p.reshape(B, S, D_MODEL)
    y_sample = y_s[M:].reshape(Bd, Td, D_MODEL)
    ckv_prompt = jnp.concatenate([jnp.broadcast_to(ckv_s[None, :M], (B, M, MLA_KV_LORA)),
                                  ckv_p.reshape(B, S, MLA_KV_LORA)], axis=1)
    krope_prompt = jnp.concatenate([jnp.broadcast_to(kr_s[None, :M], (B, M, MLA_ROPE)),
                                    kr_p.reshape(B, S, MLA_ROPE)], axis=1)
    ckv_sample = ckv_s[M:].reshape(Bd, Td, MLA_KV_LORA)
    krope_sample = kr_s[M:].reshape(Bd, Td, MLA_ROPE)
    conv_sample = a_s[M:].reshape(Bd, Td, D_FF)[:, Td - 2:]
    return (y_prompt, y_sample, ckv_prompt, krope_prompt, gla_p, conv_prompt,
            ckv_sample, krope_sample, gla_sample, conv_sample)
```

```python
import functools

import numpy as np
import jax
import jax.numpy as jnp
from jax import lax
from jax.experimental import pallas as pl
from jax.experimental.pallas import tpu as pltpu

F32 = jnp.float32
BF = jnp.bfloat16

D_MODEL = 1024
N_META = 16
GLA_HEADS = 4
GLA_DK = 64
GLA_DV = 128
GLA_RANK = 16
GLA_TAU = 16.0
GLA_CHUNK = 64
MLA_HEADS = 8
MLA_DV = 64
MLA_NOPE = 64
MLA_ROPE = 32
MLA_Q_LORA = 384
MLA_KV_LORA = 256
MLA_SCALE = (MLA_NOPE + MLA_ROPE) ** -0.5
Q_SCALE = MLA_SCALE * 1.4426950408889634
ROPE_BASE = 10000.0
D_FF = 2816
DN_ALPHA = 2.0 ** 0.25
NORM_EPS = 1e-5
NEG_INF = -1e30
PAGE = 128

C_Q, C_K, C_V, C_G, C_CQ, C_CKV, C_LAST, C_END = 0, 256, 512, 1024, 1536, 1920, 2176, 2304
LAST_ALR = 64

VMEM_LIMIT = 56 * 1024 * 1024


def _dot(a, b):
    return jnp.dot(a, b, preferred_element_type=F32)


def _dot_nt(a, b):
    return lax.dot_general(a, b, (((1,), (1,)), ((), ())), preferred_element_type=F32)


def _dot_tn(a, b):
    return lax.dot_general(a, b, (((0,), (0,)), ((), ())), preferred_element_type=F32)


def _layer_norm(x, g, b):
    mu = jnp.mean(x, axis=-1, keepdims=True)
    xc = x - mu
    var = jnp.mean(xc * xc, axis=-1, keepdims=True)
    return xc * lax.rsqrt(var + NORM_EPS) * g + b


def _rms_norm(x, g):
    ms = jnp.mean(x * x, axis=-1, keepdims=True)
    return x * lax.rsqrt(ms + NORM_EPS) * g


def _silu(x):
    return x * (1.0 / (1.0 + jnp.exp(-x)))


def _proj_kernel(x_ref, lng_ref, lnb_ref, win_ref, wg_ref, bg_ref, qng_ref, kvg_ref, wuq_ref, wuk_ref,
                 cq_ref, sq_ref, ck_ref, sk_ref,
                 qg_ref, kg_ref, vg_ref, la_ref, gg_ref, qlat_ref, qr_ref, ckv_ref, kr_ref, kcat_ref):
    h = _layer_norm(x_ref[...], lng_ref[...], lnb_ref[...])
    z = _dot(h.astype(BF), win_ref[...])
    qg_ref[...] = z[:, C_Q:C_K] * (GLA_DK ** -0.5)
    kg_ref[...] = z[:, C_K:C_V]
    vg_ref[...] = z[:, C_V:C_G].astype(BF)
    gg_ref[...] = z[:, C_G:C_CQ]
    last = z[:, C_LAST:C_END]
    y = _dot(last.astype(BF), wg_ref[...]) + bg_ref[...]
    la_ref[...] = (jnp.minimum(y, 0.0) - jnp.log1p(jnp.exp(-jnp.abs(y)))) * (1.0 / GLA_TAU)
    cqn = _rms_norm(z[:, C_CQ:C_CKV], qng_ref[...])
    qa = _dot(cqn.astype(BF), wuq_ref[...])
    qn = (qa[:, 0:512] * Q_SCALE).astype(BF)
    for hh in range(MLA_HEADS):
        pair = hh // 2
        qlat_ref[:, hh * 256:(hh + 1) * 256] = _dot(qn[:, pair * 128:(pair + 1) * 128], wuk_ref[hh]).astype(BF)
    qr = qa[:, 512:768] * cq_ref[...] + qa[:, 768:1024] * sq_ref[...]
    qr_ref[...] = (qr * Q_SCALE).astype(BF)
    ckv = _rms_norm(z[:, C_CKV:C_LAST], kvg_ref[...])
    ckv_ref[...] = ckv
    kr = last * ck_ref[...] + pltpu.roll(last, 96, 1) * sk_ref[...]
    kr_ref[...] = kr[:, 0:MLA_ROPE]
    kr2 = kr + pltpu.roll(kr, 32, 1)
    kr4 = kr2 + pltpu.roll(kr2, 64, 1)
    kcat_ref[:, 0:256] = ckv.astype(BF)
    kcat_ref[:, 256:384] = kr4.astype(BF)
    kcat_ref[:, 384:512] = kr4.astype(BF)


def _proj(x, w, tabs, tm, tab_blocks):
    T = x.shape[0]
    nt = T // tm
    cq, sq, ck, sk = tabs
    row = lambda i: (i, 0)
    tab = (lambda i: (i % tab_blocks, 0)) if tab_blocks > 1 else (lambda i: (0, 0))
    c2 = lambda i: (0, 0)
    c3 = lambda i: (0, 0, 0)
    in_specs = [
        pl.BlockSpec((tm, D_MODEL), row),
        pl.BlockSpec((1, D_MODEL), c2), pl.BlockSpec((1, D_MODEL), c2),
        pl.BlockSpec((D_MODEL, C_END), c2),
        pl.BlockSpec((128, 256), c2), pl.BlockSpec((1, 256), c2),
        pl.BlockSpec((1, MLA_Q_LORA), c2), pl.BlockSpec((1, MLA_KV_LORA), c2),
        pl.BlockSpec((MLA_Q_LORA, 1024), c2),
        pl.BlockSpec((MLA_HEADS, 128, 256), c3),
        pl.BlockSpec((tm, 256), tab), pl.BlockSpec((tm, 256), tab),
        pl.BlockSpec((tm, 128), tab), pl.BlockSpec((tm, 128), tab),
    ]
    outs = [(256, F32), (256, F32), (512, BF), (256, F32), (512, F32), (2048, BF), (256, BF), (256, F32),
            (MLA_ROPE, F32), (512, BF)]
    out_shape = [jax.ShapeDtypeStruct((T, wd), dt) for wd, dt in outs]
    out_specs = [pl.BlockSpec((tm, wd), row) for wd, _ in outs]
    return pl.pallas_call(
        _proj_kernel, out_shape=out_shape, grid=(nt,), in_specs=in_specs, out_specs=out_specs,
        compiler_params=pltpu.CompilerParams(dimension_semantics=("parallel",), vmem_limit_bytes=VMEM_LIMIT),
        name="proj",
    )(x, w["lng"], w["lnb"], w["win"], w["wg"], w["bg"], w["qng"], w["kvg"], w["wuq"], w["wuk"], cq, sq, ck, sk)


GLA_SEQ_PER_STEP = 8

def _gla_kernel(q_ref, k_ref, v_ref, la_ref, gg_ref, s0_ref, gn_ref, og_ref, sfin_ref, s_scr, *, C, n_chunks,
                independent):
    if not independent:
        i = pl.program_id(1)

        @pl.when(i == 0)
        def _():
            s_scr[...] = s0_ref[0]

        state = [s_scr[hh] for hh in range(GLA_HEADS)]

    row = lax.broadcasted_iota(jnp.int32, (C, C), 0)
    col = lax.broadcasted_iota(jnp.int32, (C, C), 1)
    tril = col <= row
    tri_b = jnp.where(tril, 1.0, 0.0).astype(BF)
    eye = (lax.broadcasted_iota(jnp.int32, (128, 128), 0) == lax.broadcasted_iota(jnp.int32, (128, 128), 1))
    lo_half = lax.broadcasted_iota(jnp.int32, (C, 128), 1) < GLA_DK
    hi_half = jnp.logical_not(lo_half)
    mid = C // 2
    gn = gn_ref[...]

    for c in range(n_chunks):
        r = slice(c * C, (c + 1) * C)
        if independent:
            state = [s0_ref[c, hh] for hh in range(GLA_HEADS)]
        g = la_ref[r, :]
        g1 = g.astype(BF)
        r1 = g - g1.astype(F32)
        g2 = r1.astype(BF)
        g3 = (r1 - g2.astype(F32)).astype(BF)
        b = _dot(tri_b, g1) + _dot(tri_b, g2) + _dot(tri_b, g3)
        b_mid = b[mid:mid + 1, :]
        b_last = b[C - 1:C, :]
        q = q_ref[r, :]
        k = k_ref[r, :]
        qe = (q * jnp.exp(b)).astype(BF)
        qm = (q * jnp.exp(b - b_mid)).astype(BF)
        km = (k * jnp.exp(b_mid - b)).astype(BF)
        kl = k * jnp.exp(b_last - b)
        dec = jnp.exp(b_last)
        for pr in range(GLA_HEADS // 2):
            pls = slice(pr * 128, (pr + 1) * 128)
            km_p = km[:, pls]
            klt = jnp.transpose(kl[:, pls]).astype(BF)
            dcol = jnp.sum(jnp.where(eye, jnp.broadcast_to(dec[:, pls], (128, 128)), 0.0),
                           axis=1, keepdims=True)
            s_pair = jnp.concatenate([state[2 * pr], state[2 * pr + 1]], axis=0).astype(BF)
            for jj in range(2):
                hh = 2 * pr + jj
                vs = slice(hh * GLA_DV, (hh + 1) * GLA_DV)
                rs = slice(jj * GLA_DK, (jj + 1) * GLA_DK)
                mine = lo_half if jj == 0 else hi_half
                v = v_ref[r, vs]
                o = _dot(jnp.where(mine, qe[:, pls], jnp.zeros((C, 128), BF)), s_pair)
                att = _dot_nt(jnp.where(mine, qm[:, pls], jnp.zeros((C, 128), BF)), km_p)
                o = o + _dot(jnp.where(tril, att, 0.0).astype(BF), v)
                state[hh] = state[hh] * dcol[rs, :] + _dot(klt[rs, :], v)
                if independent:
                    sfin_ref[c, hh] = state[hh]
                on = _rms_norm(o, gn)
                og_ref[r, vs] = (on * _silu(gg_ref[r, vs])).astype(BF)

    if not independent:
        for hh in range(GLA_HEADS):
            s_scr[hh] = state[hh]

        @pl.when(i == pl.num_programs(1) - 1)
        def _():
            for hh in range(GLA_HEADS):
                sfin_ref[0, hh] = state[hh]


def _gla(q, k, v, la, gg, s0, gn, *, B, L, C, tg, independent=False):
    nt = L // tg
    n_chunks = tg // C
    row = lambda b, i: (b * nt + i, 0)
    st = (GLA_HEADS, GLA_DK, GLA_DV)
    if independent:
        assert nt == 1
        s0_spec = pl.BlockSpec((n_chunks,) + st, lambda b, i: (b, 0, 0, 0))
        sfin_spec = s0_spec
        n_states = B * n_chunks
    else:
        s0_spec = pl.BlockSpec((1,) + st, lambda b, i: (0, 0, 0, 0))
        sfin_spec = pl.BlockSpec((1,) + st, lambda b, i: (b, 0, 0, 0))
        n_states = B
    in_specs = [pl.BlockSpec((tg, 256), row), pl.BlockSpec((tg, 256), row), pl.BlockSpec((tg, 512), row),
                pl.BlockSpec((tg, 256), row), pl.BlockSpec((tg, 512), row),
                s0_spec, pl.BlockSpec((1, GLA_DV), lambda b, i: (0, 0))]
    out_shape = [jax.ShapeDtypeStruct((B * L, 512), BF), jax.ShapeDtypeStruct((n_states,) + st, F32)]
    out_specs = [pl.BlockSpec((tg, 512), row), sfin_spec]
    return pl.pallas_call(
        functools.partial(_gla_kernel, C=C, n_chunks=n_chunks, independent=independent),
        out_shape=out_shape, grid=(B, nt), in_specs=in_specs, out_specs=out_specs,
        scratch_shapes=[pltpu.VMEM(st, F32)],
        compiler_params=pltpu.CompilerParams(dimension_semantics=("parallel", "arbitrary"),
                                             vmem_limit_bytes=VMEM_LIMIT),
        name="gla_c%d%s" % (C, "_ind" if independent else ""),
    )(q, k, v, la, gg, s0, gn)


ATTN_ROW_BLOCK = 2048
ATTN_TQ = 256


def _lanes(x, n):
    if n % 128 == 0:
        return x if n == 128 else jnp.concatenate([x] * (n // 128), axis=1)
    return x[:, 0:n]


def _attn_kernel(qlat_ref, qr_ref, kcat_ref, *rest, tq, has_meta):
    if has_meta:
        kmeta_ref, o_ref, q_scr, s_scr, v_scr, m_scr, l_scr, acc_scr = rest
    else:
        o_ref, q_scr, s_scr, v_scr, m_scr, l_scr, acc_scr = rest
    i = pl.program_id(1)
    R = MLA_HEADS * tq
    RB = min(R, ATTN_ROW_BLOCK)
    nk = s_scr.shape[2]
    lane_head = lax.broadcasted_iota(jnp.int32, (tq, 256), 1) // MLA_ROPE
    qr = qr_ref[...]
    for hh in range(MLA_HEADS):
        rows = slice(hh * tq, (hh + 1) * tq)
        q_scr[rows, 0:256] = qlat_ref[:, hh * 256:(hh + 1) * 256]
        q_scr[rows, 256:512] = jnp.where(lane_head == hh, qr, jnp.zeros_like(qr))
    m_scr[2] = jnp.full(m_scr.shape[1:], NEG_INF, F32)
    l_scr[...] = jnp.zeros(l_scr.shape, F32)
    acc_scr[...] = jnp.zeros(acc_scr.shape, F32)

    def score(kc, t, mask):
        slot = t % 2
        v_scr[slot] = kc[:, 0:256]
        for rb in range(R // RB):
            rows = slice(rb * RB, (rb + 1) * RB)
            s = _dot_nt(q_scr[rows, :], kc)
            key = lax.broadcasted_iota(jnp.int32, (RB, nk), 1)
            if mask == "causal":
                tok = lax.broadcasted_iota(jnp.int32, (RB, nk), 0) % tq
                s = jnp.where(key <= tok, s, NEG_INF)
            elif mask == "meta":
                s = jnp.where(key < N_META, s, NEG_INF)
            s_scr[slot, rows, :] = s
            m_scr[t % 3, rows, :] = jnp.maximum(m_scr[(t + 2) % 3, rows, :], jnp.max(s, axis=1, keepdims=True))

    def update(t):
        slot = t % 2
        vb = v_scr[slot]
        for rb in range(R // RB):
            rows = slice(rb * RB, (rb + 1) * RB)
            m_new = m_scr[t % 3, rows, :]
            alpha = jnp.exp2(m_scr[(t + 2) % 3, rows, :] - m_new)
            p = jnp.exp2(s_scr[slot, rows, :] - _lanes(m_new, nk))
            psum = p if nk < 128 else sum(p[:, c * 128:(c + 1) * 128] for c in range(nk // 128))
            l_scr[rows, 0:psum.shape[1]] = alpha[:, 0:psum.shape[1]] * l_scr[rows, 0:psum.shape[1]] + psum
            acc_scr[rows, :] = _lanes(alpha, 256) * acc_scr[rows, :] + _dot(p.astype(BF), vb)

    score(kcat_ref[pl.ds(pl.multiple_of(i * tq, tq), tq), :], 0, "causal")
    if has_meta:
        score(kmeta_ref[...], 1, "meta")
        update(0)

        def full(j, carry):
            update(j + 1)
            score(kcat_ref[pl.ds(pl.multiple_of(j * tq, tq), tq), :], j + 2, None)
            return carry

        lax.fori_loop(0, i, full, 0)
        update(i + 1)
    else:
        update(0)
    for hh in range(MLA_HEADS):
        rows = slice(hh * tq, (hh + 1) * tq)
        o = acc_scr[rows, :] * (1.0 / jnp.sum(l_scr[rows, :], axis=1, keepdims=True))
        o_ref[:, hh * 256:(hh + 1) * 256] = o.astype(BF)


def _attn(qlat, qr, kcat, kmeta, *, B, L, tq):
    nq = L // tq
    row = lambda b, i: (b * nq + i, 0)
    in_specs = [pl.BlockSpec((tq, 2048), row), pl.BlockSpec((tq, 256), row),
                pl.BlockSpec((L, 512), lambda b, i: (b, 0))]
    args = [qlat, qr, kcat]
    if kmeta is not None:
        in_specs.append(pl.BlockSpec((tq, 512), lambda b, i: (0, 0)))
        args.append(jnp.pad(kmeta, ((0, tq - N_META), (0, 0))))
    else:
        assert nq == 1
    R = MLA_HEADS * tq
    return pl.pallas_call(
        functools.partial(_attn_kernel, tq=tq, has_meta=kmeta is not None),
        out_shape=jax.ShapeDtypeStruct((B * L, 2048), BF), grid=(B, nq), in_specs=in_specs,
        out_specs=pl.BlockSpec((tq, 2048), row),
        scratch_shapes=[pltpu.VMEM((R, 512), BF), pltpu.VMEM((2, R, tq), F32), pltpu.VMEM((2, tq, 256), BF),
                        pltpu.VMEM((3, R, 128), F32), pltpu.VMEM((R, 128), F32), pltpu.VMEM((R, 256), F32)],
        compiler_params=pltpu.CompilerParams(dimension_semantics=("parallel", "arbitrary"),
                                             vmem_limit_bytes=VMEM_LIMIT),
        name="attn_t%d" % tq,
    )(*args)


NEW_PAD = 16
KEY_CH = 1024
KEY_GROUP = 8192
PATTN_SEQ = 2


def _pattn_kernel(pt_ref, ql_ref, qr_ref, cn_ref, kn_ref, cckv_hbm, ckr_hbm, o_ref, ckv_buf, kr_buf, kb_scr, s_scr,
                  sem, *, n_pages, t_new):
    g = pl.program_id(0)
    ng = pl.num_programs(0)
    slot = g % 2
    n_keys = n_pages * PAGE

    def copies(step, sl):
        out = []
        for j in range(PATTN_SEQ):
            for pg in range(n_pages):
                page = pt_ref[step * PATTN_SEQ + j, pg]
                out.append(pltpu.make_async_copy(
                    cckv_hbm.at[page], ckv_buf.at[sl * PATTN_SEQ + j, pl.ds(pg * PAGE, PAGE), :], sem.at[0, sl]))
                out.append(pltpu.make_async_copy(
                    ckr_hbm.at[page], kr_buf.at[sl * PATTN_SEQ + j, :, pl.ds(pg * PAGE, PAGE)], sem.at[1, sl]))
        return out

    @pl.when(g == 0)
    def _():
        for c in copies(0, 0):
            c.start()

    for c in copies(jnp.minimum(g + 1, ng - 1), 1 - slot):
        c.start()
    for c in copies(g, slot):
        c.wait()

    for j in range(PATTN_SEQ):
        buf = slot * PATTN_SEQ + j
        ql = ql_ref[j]
        qr = qr_ref[j]
        R = ql.shape[0]
        cn = cn_ref[j].astype(BF)
        kn = kn_ref[j].astype(BF)
        s_new = _dot_nt(ql, cn) + _dot_nt(qr, kn)
        tok = lax.broadcasted_iota(jnp.int32, (R, NEW_PAD), 0) % t_new
        key = lax.broadcasted_iota(jnp.int32, (R, NEW_PAD), 1)
        s_new = jnp.where(key <= tok, s_new, NEG_INF)
        m_new = jnp.max(s_new, axis=1, keepdims=True)
        p_new = jnp.exp2(s_new - m_new)
        parts = [(m_new, jnp.sum(p_new, axis=1, keepdims=True), _dot(p_new.astype(BF), cn))]
        for grp in range(n_keys // KEY_GROUP):
            for ch in range(grp * (KEY_GROUP // KEY_CH), (grp + 1) * (KEY_GROUP // KEY_CH)):
                ks = slice(ch * KEY_CH, (ch + 1) * KEY_CH)
                kcb = ckv_buf[buf, ks, :].astype(BF)
                kb_scr[j, ks, :] = kcb
                s_scr[j, :, ks] = _dot_nt(ql, kcb) + _dot(qr, kr_buf[buf, :, ks].astype(BF))
            gs = slice(grp * KEY_GROUP, (grp + 1) * KEY_GROUP)
            s_g = s_scr[j, :, gs]
            m_g = jnp.max(s_g, axis=1, keepdims=True)
            p_g = jnp.exp2(s_g - m_g)
            parts.append((m_g, jnp.sum(p_g, axis=1, keepdims=True), _dot(p_g.astype(BF), kb_scr[j, gs, :])))
        m = functools.reduce(jnp.maximum, [pt[0] for pt in parts])
        l = sum(pt[1] * jnp.exp2(pt[0] - m) for pt in parts)
        acc = sum(pt[2] * jnp.exp2(pt[0] - m) for pt in parts)
        o_ref[j] = acc * (1.0 / l)

    @pl.when(g == ng - 1)
    def _():
        for c in copies(g, 1 - slot):
            c.wait()


def _pattn(page_table, ql, qr, cn, kn, cache_ckv, cache_krope_t, *, t_new):
    Bd, n_pages = page_table.shape
    R = ql.shape[1]
    n_keys = n_pages * PAGE
    G = PATTN_SEQ
    blk = lambda b, pt: (b, 0, 0)
    grid_spec = pltpu.PrefetchScalarGridSpec(
        num_scalar_prefetch=1, grid=(Bd // G,),
        in_specs=[pl.BlockSpec((G, R, MLA_KV_LORA), blk), pl.BlockSpec((G, R, MLA_ROPE), blk),
                  pl.BlockSpec((G, NEW_PAD, MLA_KV_LORA), blk), pl.BlockSpec((G, NEW_PAD, MLA_ROPE), blk),
                  pl.BlockSpec(memory_space=pl.ANY), pl.BlockSpec(memory_space=pl.ANY)],
        out_specs=pl.BlockSpec((G, R, MLA_KV_LORA), blk),
        scratch_shapes=[pltpu.VMEM((2 * G, n_keys, MLA_KV_LORA), F32), pltpu.VMEM((2 * G, MLA_ROPE, n_keys), F32),
                        pltpu.VMEM((G, n_keys, MLA_KV_LORA), BF), pltpu.VMEM((G, R, n_keys), F32),
                        pltpu.SemaphoreType.DMA((2, 2))])
    return pl.pallas_call(
        functools.partial(_pattn_kernel, n_pages=n_pages, t_new=t_new),
        out_shape=jax.ShapeDtypeStruct((Bd, R, MLA_KV_LORA), F32), grid_spec=grid_spec,
        compiler_params=pltpu.CompilerParams(dimension_semantics=("arbitrary",), vmem_limit_bytes=VMEM_LIMIT),
        name="pattn",
    )(page_table, ql, qr, cn, kn, cache_ckv, cache_krope_t)


FF_CH = 256
N_FF_CH = D_FF // FF_CH


def _ffn_kernel(x_ref, og_ref, ol_ref, lng_ref, lnb_ref, wuv_ref, wo_ref, l1g_ref, l1b_ref, wf_ref,
                cw_ref, cb_ref, wd_ref, l2g_ref, l2b_ref, *rest, tm, inject):
    if inject:
        inj1_ref, inj2_ref, m1_ref, m2_ref, y_ref, a_ref, act_scr = rest
    else:
        halo0_ref, y_ref, alast_ref, act_scr, halo_scr = rest
        i = pl.program_id(1)

        @pl.when(i == 0)
        def _():
            halo_scr[...] = halo0_ref[...]

    h = _layer_norm(x_ref[...], lng_ref[...], lnb_ref[...])
    om = jnp.concatenate([_dot(ol_ref[:, j * 512:(j + 1) * 512], wuv_ref[j]) for j in range(MLA_HEADS // 2)],
                         axis=1)
    mix = _dot(og_ref[...], wo_ref[0:512, :]) + _dot(om.astype(BF), wo_ref[512:1024, :])
    h1 = _layer_norm(DN_ALPHA * h + mix, l1g_ref[...], l1b_ref[...])
    h1b = h1.astype(BF)
    ridx = lax.broadcasted_iota(jnp.int32, (tm, FF_CH), 0)
    for c in range(N_FF_CH):
        cs = slice(c * FF_CH, (c + 1) * FF_CH)
        a = _dot(h1b, wf_ref[:, cs])
        up = _dot(h1b, wf_ref[:, D_FF + c * FF_CH:D_FF + (c + 1) * FF_CH])
        r1 = pltpu.roll(a, 1, 0)
        r2 = pltpu.roll(a, 2, 0)
        if inject:
            a1 = jnp.where(m1_ref[...] > 0.0, inj1_ref[:, cs], r1)
            a2 = jnp.where(m2_ref[...] > 0.0, inj2_ref[:, cs], r2)
            a_ref[:, cs] = a
        else:
            hl = halo_scr[:, cs]
            a1 = jnp.where(ridx == 0, hl[1:2, :], r1)
            a2 = jnp.where(ridx == 0, hl[0:1, :], jnp.where(ridx == 1, hl[1:2, :], r2))
            halo_scr[:, cs] = a[tm - 2:tm, :]
            alast_ref[0, :, cs] = a[tm - 2:tm, :]
        conv = cb_ref[:, cs] + cw_ref[0:1, cs] * a2 + cw_ref[1:2, cs] * a1 + cw_ref[2:3, cs] * a
        act_scr[:, cs] = (_silu(conv) * up).astype(BF)
    ffn = _dot(act_scr[...], wd_ref[...])
    y_ref[...] = _layer_norm(DN_ALPHA * h1 + ffn, l2g_ref[...], l2b_ref[...])


def _ffn(x, og, ol, w, *, B, L, tm, halo0=None, inject=None):
    nt = L // tm
    T = B * L
    row = lambda b, i: (b * nt + i, 0)
    c2 = lambda b, i: (0, 0)
    c3 = lambda b, i: (0, 0, 0)
    once = pl.Buffered(1)
    in_specs = [pl.BlockSpec((tm, D_MODEL), row), pl.BlockSpec((tm, 512), row), pl.BlockSpec((tm, 2048), row),
                pl.BlockSpec((1, D_MODEL), c2), pl.BlockSpec((1, D_MODEL), c2),
                pl.BlockSpec((MLA_HEADS // 2, 512, 128), c3, pipeline_mode=once),
                pl.BlockSpec((D_MODEL, D_MODEL), c2, pipeline_mode=once),
                pl.BlockSpec((1, D_MODEL), c2), pl.BlockSpec((1, D_MODEL), c2),
                pl.BlockSpec((D_MODEL, 2 * D_FF), c2, pipeline_mode=once),
                pl.BlockSpec((3, D_FF), c2), pl.BlockSpec((1, D_FF), c2),
                pl.BlockSpec((D_FF, D_MODEL), c2, pipeline_mode=once),
                pl.BlockSpec((1, D_MODEL), c2), pl.BlockSpec((1, D_MODEL), c2)]
    args = [x, og, ol, w["lng"], w["lnb"], w["wuv"], w["wo"], w["l1g"], w["l1b"], w["wf"],
            w["cw"], w["cb"], w["wd"], w["l2g"], w["l2b"]]
    scratch = [pltpu.VMEM((tm, D_FF), BF)]
    if inject is not None:
        inj1, inj2, m1, m2 = inject
        in_specs += [pl.BlockSpec((tm, D_FF), row), pl.BlockSpec((tm, D_FF), row),
                     pl.BlockSpec((tm, 1), row), pl.BlockSpec((tm, 1), row)]
        args += [inj1, inj2, m1, m2]
        out_shape = [jax.ShapeDtypeStruct((T, D_MODEL), F32), jax.ShapeDtypeStruct((T, D_FF), F32)]
        out_specs = [pl.BlockSpec((tm, D_MODEL), row), pl.BlockSpec((tm, D_FF), row)]
    else:
        in_specs.append(pl.BlockSpec((2, D_FF), c2))
        args.append(halo0)
        out_shape = [jax.ShapeDtypeStruct((T, D_MODEL), F32), jax.ShapeDtypeStruct((B, 2, D_FF), F32)]
        out_specs = [pl.BlockSpec((tm, D_MODEL), row), pl.BlockSpec((1, 2, D_FF), lambda b, i: (b, 0, 0))]
        scratch.append(pltpu.VMEM((2, D_FF), F32))
    return pl.pallas_call(
        functools.partial(_ffn_kernel, tm=tm, inject=inject is not None),
        out_shape=out_shape, grid=(B, nt), in_specs=in_specs, out_specs=out_specs, scratch_shapes=scratch,
        compiler_params=pltpu.CompilerParams(dimension_semantics=("parallel", "arbitrary"),
                                             vmem_limit_bytes=VMEM_LIMIT),
        name="ffn_inject" if inject is not None else "ffn_seq",
    )(*args)


def _rope_tables(pos):
    half = MLA_ROPE // 2
    inv = ROPE_BASE ** (-jnp.arange(half, dtype=F32) / half)
    ang = pos.astype(F32)[:, None] * inv[None, :]
    cos = jnp.concatenate([jnp.cos(ang)] * 2, axis=1)
    sin = jnp.concatenate([jnp.sin(ang)] * 2, axis=1)
    pad = jnp.zeros((pos.shape[0], 128 - MLA_ROPE), F32)
    return (jnp.tile(cos, (1, MLA_HEADS)), jnp.tile(sin, (1, MLA_HEADS)),
            jnp.concatenate([cos, pad], axis=1), jnp.concatenate([sin, pad], axis=1))


def _swap_halves(w):
    half = w.shape[-1] // 2
    return jnp.concatenate([-w[..., half:], w[..., :half]], axis=-1)


def _prep_weights(ln_emb_g, ln_emb_b, w_in, w_gate_up, b_gate, gla_norm_g, mla_q_norm_g, mla_kv_norm_g,
                  w_uq, w_uk, w_uv, w_o, ln1_g, ln1_b, w_ffn_in, conv_w, conv_b, w_down, ln2_g, ln2_b):
    o_alr, o_cq, o_ckv, o_kr = 1536, 1552, 1936, 2192
    w_kr = w_in[:, o_kr:o_kr + MLA_ROPE]
    last = jnp.concatenate([w_kr, _swap_halves(w_kr), w_in[:, o_alr:o_cq],
                            jnp.zeros((D_MODEL, 128 - 2 * MLA_ROPE - GLA_RANK), F32)], axis=1)
    win = jnp.concatenate([w_in[:, :o_alr], w_in[:, o_cq:o_kr], last], axis=1).astype(BF)
    wg = jnp.zeros((128, 256), F32).at[LAST_ALR:LAST_ALR + GLA_RANK].set(w_gate_up).astype(BF)
    uq = w_uq.reshape(MLA_Q_LORA, MLA_HEADS, MLA_NOPE + MLA_ROPE)
    uq_r = uq[:, :, MLA_NOPE:]
    wuq = jnp.concatenate([uq[:, :, :MLA_NOPE].reshape(MLA_Q_LORA, 512), uq_r.reshape(MLA_Q_LORA, 256),
                           _swap_halves(uq_r).reshape(MLA_Q_LORA, 256)], axis=1).astype(BF)
    ukt = jnp.transpose(w_uk.reshape(MLA_KV_LORA, MLA_HEADS, MLA_NOPE), (1, 2, 0))
    z64 = jnp.zeros_like(ukt)
    even = (jnp.arange(MLA_HEADS) % 2 == 0)[:, None, None]
    wuk = jnp.concatenate([jnp.where(even, ukt, z64), jnp.where(even, z64, ukt)], axis=1).astype(BF)
    uv = jnp.transpose(w_uv.reshape(MLA_KV_LORA, MLA_HEADS, MLA_DV), (1, 0, 2))
    uv = uv.reshape(MLA_HEADS // 2, 2, MLA_KV_LORA, MLA_DV)
    zz = jnp.zeros_like(uv[:, 0])
    wuv = jnp.concatenate([jnp.concatenate([uv[:, 0], zz], axis=2), jnp.concatenate([zz, uv[:, 1]], axis=2)],
                          axis=1).astype(BF)
    r = lambda a: a.reshape(1, -1)
    return dict(lng=r(ln_emb_g), lnb=r(ln_emb_b), win=win, wg=wg, bg=r(b_gate), qng=r(mla_q_norm_g),
                kvg=r(mla_kv_norm_g), wuq=wuq, wuk=wuk, wuv=wuv, wo=w_o.astype(BF), l1g=r(ln1_g), l1b=r(ln1_b),
                wf=w_ffn_in.astype(BF), cw=conv_w, cb=r(conv_b),
                wd=w_down.astype(BF), l2g=r(ln2_g), l2b=r(ln2_b), gn=r(gla_norm_g))


def kernel(x_prompt, x_sample, cache_ckv, cache_krope, page_table, state_gla, state_conv, meta_tokens, ln_emb_g, ln_emb_b, w_in, w_gate_up, b_gate, gla_norm_g, mla_q_norm_g, mla_kv_norm_g, w_uq, w_uk, w_uv, w_o, ln1_g, ln1_b, w_ffn_in, conv_w, conv_b, w_down, ln2_g, ln2_b):
    B, S, _ = x_prompt.shape
    Bd, Td, _ = x_sample.shape
    past_len = page_table.shape[1] * PAGE
    n_small = N_META + Bd * Td
    w = _prep_weights(ln_emb_g, ln_emb_b, w_in, w_gate_up, b_gate, gla_norm_g, mla_q_norm_g, mla_kv_norm_g,
                      w_uq, w_uk, w_uv, w_o, ln1_g, ln1_b, w_ffn_in, conv_w, conv_b, w_down, ln2_g, ln2_b)

    xs = jnp.concatenate([meta_tokens.astype(F32), x_sample.reshape(Bd * Td, D_MODEL)], axis=0)
    pos_s = jnp.concatenate([jnp.arange(N_META, dtype=jnp.int32),
                             jnp.tile(past_len + jnp.arange(Td, dtype=jnp.int32), Bd)])
    (qg_s, kg_s, vg_s, la_s, gg_s, qlat_s, qr_s, ckv_s, kr_s, kcat_s) = _proj(
        xs, w, _rope_tables(pos_s), n_small, 1)

    M = N_META
    zero_state = jnp.zeros((1, GLA_HEADS, GLA_DK, GLA_DV), F32)
    og_m, s_meta = _gla(qg_s[:M], kg_s[:M], vg_s[:M], la_s[:M], gg_s[:M], zero_state, w["gn"],
                        B=1, L=M, C=M, tg=M)
    ol_m = _attn(qlat_s[:M], qr_s[:M], kcat_s[:M], None, B=1, L=M, tq=M)

    TP = 16
    padt = lambda a: jnp.pad(a[M:].reshape(Bd, Td, -1), ((0, 0), (0, TP - Td), (0, 0))).reshape(Bd * TP, -1)
    og_sp, gla_sample = _gla(padt(qg_s), padt(kg_s), padt(vg_s), padt(la_s), padt(gg_s), state_gla, w["gn"],
                             B=Bd // GLA_SEQ_PER_STEP, L=GLA_SEQ_PER_STEP * TP, C=TP, tg=GLA_SEQ_PER_STEP * TP,
                             independent=True)
    og_s = og_sp.reshape(Bd, TP, 512)[:, :Td].reshape(Bd * Td, 512)

    heads_first = lambda a, d: jnp.transpose(a[M:].reshape(Bd, Td, MLA_HEADS, d), (0, 2, 1, 3)).reshape(
        Bd, MLA_HEADS * Td, d)
    padn = lambda a: jnp.pad(a[M:].reshape(Bd, Td, -1), ((0, 0), (0, NEW_PAD - Td), (0, 0)))
    o_s = _pattn(page_table, heads_first(qlat_s, MLA_KV_LORA), heads_first(qr_s, MLA_ROPE),
                 padn(ckv_s), padn(kr_s), cache_ckv, jnp.swapaxes(cache_krope, 1, 2), t_new=Td)
    ol_s = jnp.transpose(o_s.reshape(Bd, MLA_HEADS, Td, MLA_KV_LORA), (0, 2, 1, 3)).reshape(
        Bd * Td, MLA_HEADS * MLA_KV_LORA).astype(BF)

    sc = state_conv.astype(F32)
    zrow = jnp.zeros((Bd, 1, D_FF), F32)
    inj1 = jnp.concatenate([sc[:, 1:2], zrow, zrow, zrow], axis=1).reshape(Bd * Td, D_FF)
    inj2 = jnp.concatenate([sc[:, 0:1], sc[:, 1:2], zrow, zrow], axis=1).reshape(Bd * Td, D_FF)
    zmeta = jnp.zeros((M, D_FF), F32)
    t_in = np.concatenate([np.arange(M), np.tile(np.arange(Td), Bd)])
    m1 = jnp.asarray((t_in == 0).astype(np.float32)[:, None])
    m2 = jnp.asarray((t_in <= 1).astype(np.float32)[:, None])
    y_s, a_s = _ffn(xs, jnp.concatenate([og_m, og_s], axis=0), jnp.concatenate([ol_m, ol_s], axis=0), w,
                    B=1, L=n_small, tm=176,
                    inject=(jnp.concatenate([zmeta, inj1], axis=0), jnp.concatenate([zmeta, inj2], axis=0), m1, m2))

    TM = 512
    xp = x_prompt.reshape(B * S, D_MODEL)
    pos_p = N_META + jnp.arange(S, dtype=jnp.int32)
    (qg_p, kg_p, vg_p, la_p, gg_p, qlat_p, qr_p, ckv_p, kr_p, kcat_p) = _proj(
        xp, w, _rope_tables(pos_p), TM, S // TM)
    og_p, gla_p = _gla(qg_p, kg_p, vg_p, la_p, gg_p, s_meta, w["gn"], B=B, L=S, C=GLA_CHUNK, tg=TM)
    ol_p = _attn(qlat_p, qr_p, kcat_p, kcat_s[:M], B=B, L=S, tq=ATTN_TQ)
    y_p, conv_prompt = _ffn(xp, og_p, ol_p, w, B=B, L=S, tm=TM, halo0=a_s[M - 2:M])

    y_prompt = y_p.reshape(B, S, D_MODEL)
    y_sample = y_s[M:].reshape(Bd, Td, D_MODEL)
    ckv_prompt = jnp.concatenate([jnp.broadcast_to(ckv_s[None, :M], (B, M, MLA_KV_LORA)),
                                  ckv_p.reshape(B, S, MLA_KV_LORA)], axis=1)
    krope_prompt = jnp.concatenate([jnp.broadcast_to(kr_s[None, :M], (B, M, MLA_ROPE)),
                                    kr_p.reshape(B, S, MLA_ROPE)], axis=1)
    ckv_sample = ckv_s[M:].reshape(Bd, Td, MLA_KV_LORA)
    krope_sample = kr_s[M:].reshape(Bd, Td, MLA_ROPE)
    conv_sample = a_s[M:].reshape(Bd, Td, D_FF)[:, Td - 2:]
    return (y_prompt, y_sample, ckv_prompt, krope_prompt, gla_p, conv_prompt,
            ckv_sample, krope_sample, gla_sample, conv_sample)
```

```python
import functools

import numpy as np
import jax
import jax.numpy as jnp
from jax import lax
from jax.experimental import pallas as pl
from jax.experimental.pallas import tpu as pltpu

F32 = jnp.float32
BF = jnp.bfloat16

D_MODEL = 1024
N_META = 16
GLA_HEADS = 4
GLA_DK = 64
GLA_DV = 128
GLA_RANK = 16
GLA_TAU = 16.0
GLA_CHUNK = 64
MLA_HEADS = 8
MLA_DV = 64
MLA_NOPE = 64
MLA_ROPE = 32
MLA_Q_LORA = 384
MLA_KV_LORA = 256
MLA_SCALE = (MLA_NOPE + MLA_ROPE) ** -0.5
Q_SCALE = MLA_SCALE * 1.4426950408889634
ROPE_BASE = 10000.0
D_FF = 2816
DN_ALPHA = 2.0 ** 0.25
NORM_EPS = 1e-5
NEG_INF = -1e30
PAGE = 128

C_Q, C_K, C_V, C_G, C_CQ, C_CKV, C_LAST, C_END = 0, 256, 512, 1024, 1536, 1920, 2176, 2304
LAST_ALR = 64

VMEM_LIMIT = 56 * 1024 * 1024


def _dot(a, b):
    return jnp.dot(a, b, preferred_element_type=F32)


def _dot_nt(a, b):
    return lax.dot_general(a, b, (((1,), (1,)), ((), ())), preferred_element_type=F32)


def _dot_tn(a, b):
    return lax.dot_general(a, b, (((0,), (0,)), ((), ())), preferred_element_type=F32)


def _layer_norm(x, g, b):
    mu = jnp.mean(x, axis=-1, keepdims=True)
    xc = x - mu
    var = jnp.mean(xc * xc, axis=-1, keepdims=True)
    return xc * lax.rsqrt(var + NORM_EPS) * g + b


def _rms_norm(x, g):
    ms = jnp.mean(x * x, axis=-1, keepdims=True)
    return x * lax.rsqrt(ms + NORM_EPS) * g


def _silu(x):
    return x * (1.0 / (1.0 + jnp.exp(-x)))


def _proj_kernel(x_ref, lng_ref, lnb_ref, win_ref, wg_ref, bg_ref, qng_ref, kvg_ref, wuq_ref, wuk_ref,
                 cq_ref, sq_ref, ck_ref, sk_ref,
                 qg_ref, kg_ref, vg_ref, la_ref, gg_ref, qlat_ref, qr_ref, ckv_ref, kr_ref, kcat_ref):
    h = _layer_norm(x_ref[...], lng_ref[...], lnb_ref[...])
    z = _dot(h.astype(BF), win_ref[...])
    qg_ref[...] = z[:, C_Q:C_K] * (GLA_DK ** -0.5)
    kg_ref[...] = z[:, C_K:C_V]
    vg_ref[...] = z[:, C_V:C_G].astype(BF)
    gg_ref[...] = z[:, C_G:C_CQ]
    last = z[:, C_LAST:C_END]
    y = _dot(last.astype(BF), wg_ref[...]) + bg_ref[...]
    la_ref[...] = (jnp.minimum(y, 0.0) - jnp.log1p(jnp.exp(-jnp.abs(y)))) * (1.0 / GLA_TAU)
    cqn = _rms_norm(z[:, C_CQ:C_CKV], qng_ref[...])
    qa = _dot(cqn.astype(BF), wuq_ref[...])
    qn = (qa[:, 0:512] * Q_SCALE).astype(BF)
    for pair in range(MLA_HEADS // 2):
        qlat_ref[:, pair * 512:(pair + 1) * 512] = _dot(qn[:, pair * 128:(pair + 1) * 128], wuk_ref[pair]).astype(BF)
    qr = qa[:, 512:768] * cq_ref[...] + qa[:, 768:1024] * sq_ref[...]
    qr_ref[...] = (qr * Q_SCALE).astype(BF)
    ckv = _rms_norm(z[:, C_CKV:C_LAST], kvg_ref[...])
    ckv_ref[...] = ckv
    kr = last * ck_ref[...] + pltpu.roll(last, 96, 1) * sk_ref[...]
    kr_ref[...] = kr[:, 0:MLA_ROPE]
    kr2 = kr + pltpu.roll(kr, 32, 1)
    kr4 = kr2 + pltpu.roll(kr2, 64, 1)
    kcat_ref[:, 0:256] = ckv.astype(BF)
    kcat_ref[:, 256:384] = kr4.astype(BF)
    kcat_ref[:, 384:512] = kr4.astype(BF)


def _proj(x, w, tabs, *, B, L, tm):
    T = B * L
    nt = L // tm
    cq, sq, ck, sk = tabs
    row = lambda b, i: (b * nt + i, 0)
    tab = lambda b, i: (i, 0)
    c2 = lambda b, i: (0, 0)
    c3 = lambda b, i: (0, 0, 0)
    in_specs = [
        pl.BlockSpec((tm, D_MODEL), row),
        pl.BlockSpec((1, D_MODEL), c2), pl.BlockSpec((1, D_MODEL), c2),
        pl.BlockSpec((D_MODEL, C_END), c2),
        pl.BlockSpec((128, 256), c2), pl.BlockSpec((1, 256), c2),
        pl.BlockSpec((1, MLA_Q_LORA), c2), pl.BlockSpec((1, MLA_KV_LORA), c2),
        pl.BlockSpec((MLA_Q_LORA, 1024), c2),
        pl.BlockSpec((MLA_HEADS // 2, 128, 512), c3),
        pl.BlockSpec((tm, 256), tab), pl.BlockSpec((tm, 256), tab),
        pl.BlockSpec((tm, 128), tab), pl.BlockSpec((tm, 128), tab),
    ]
    args = [x, w["lng"], w["lnb"], w["win"], w["wg"], w["bg"], w["qng"], w["kvg"], w["wuq"], w["wuk"], cq, sq, ck, sk]
    outs = [(256, F32), (256, F32), (512, BF), (256, F32), (512, F32), (2048, BF), (256, BF), (256, F32),
            (MLA_ROPE, F32), (512, BF)]
    out_shape = [jax.ShapeDtypeStruct((T, wd), dt) for wd, dt in outs]
    out_specs = [pl.BlockSpec((tm, wd), row) for wd, _ in outs]
    return pl.pallas_call(
        _proj_kernel, out_shape=out_shape, grid=(B, nt), in_specs=in_specs, out_specs=out_specs,
        compiler_params=pltpu.CompilerParams(dimension_semantics=("parallel", "parallel"),
                                             vmem_limit_bytes=VMEM_LIMIT),
        name="proj",
    )(*args)


GLA_SEQ_PER_STEP = 8
GLA_ROWS_PER_STEP = 2
GLA_PAIR_MERGE_ROWS = 32


def _gla_consts(C):
    row = lax.broadcasted_iota(jnp.int32, (2 * C, 2 * C), 0)
    col = lax.broadcasted_iota(jnp.int32, (2 * C, 2 * C), 1)
    tril2 = (col <= row) & (col >= (row // C) * C)
    eye = (lax.broadcasted_iota(jnp.int32, (128, 128), 0) == lax.broadcasted_iota(jnp.int32, (128, 128), 1))
    first_half = ((lax.broadcasted_iota(jnp.int32, (2 * C, 128), 1) < GLA_DK)
                  == (lax.broadcasted_iota(jnp.int32, (2 * C, 128), 0) < C))
    return C, tril2, eye, first_half


def _gla_cumsum(g, C):
    n = g.shape[0]
    row = lax.broadcasted_iota(jnp.int32, (n, n), 0)
    col = lax.broadcasted_iota(jnp.int32, (n, n), 1)
    tri = jnp.where((col <= row) & (col >= (row // C) * C), 1.0, 0.0).astype(BF)
    g1 = g.astype(BF)
    r1 = g - g1.astype(F32)
    g2 = r1.astype(BF)
    g3 = (r1 - g2.astype(F32)).astype(BF)
    return _dot(tri, g1) + _dot(tri, g2) + _dot(tri, g3)


def _gla_chunk(b, q, k, v, gg, state, gn, consts):
    C, tril2, eye, own_lanes = consts
    mid = C // 2
    b_mid = b[mid:mid + 1, :]
    b_last = b[C - 1:C, :]
    qe = (q * jnp.exp(b)).astype(BF)
    qm = (q * jnp.exp(b - b_mid)).astype(BF)
    km = (k * jnp.exp(b_mid - b)).astype(BF)
    kl = k * jnp.exp(b_last - b)
    dec = jnp.exp(b_last)
    out = []
    zeros = jnp.zeros((2 * C, 128), BF)
    for pr in range(GLA_HEADS // 2):
        pls = slice(pr * 128, (pr + 1) * 128)
        v0 = v[:, (2 * pr) * GLA_DV:(2 * pr + 1) * GLA_DV]
        v1 = v[:, (2 * pr + 1) * GLA_DV:(2 * pr + 2) * GLA_DV]
        merged = 2 * C <= GLA_PAIR_MERGE_ROWS
        qm2 = jnp.where(own_lanes, jnp.concatenate([qm[:, pls], qm[:, pls]], axis=0), zeros)
        qe2 = jnp.where(own_lanes, jnp.concatenate([qe[:, pls], qe[:, pls]], axis=0), zeros)
        if merged:
            km2 = jnp.concatenate([km[:, pls], km[:, pls]], axis=0)
            att2 = jnp.where(tril2, _dot_nt(qm2, km2), 0.0).astype(BF)
        s_pair = jnp.concatenate([state[2 * pr], state[2 * pr + 1]], axis=0).astype(BF)
        klt = jnp.transpose(kl[:, pls]).astype(BF)
        dcol = jnp.sum(jnp.where(eye, jnp.broadcast_to(dec[:, pls], (128, 128)), 0.0),
                       axis=1, keepdims=True)
        if merged:
            o2 = _dot(jnp.concatenate([qe2, att2], axis=1), jnp.concatenate([s_pair, v0, v1], axis=0))
            kv2 = _dot(klt, v[:, (2 * pr) * GLA_DV:(2 * pr + 2) * GLA_DV])
        for jj in range(2):
            hh = 2 * pr + jj
            rs = slice(jj * GLA_DK, (jj + 1) * GLA_DK)
            hr = slice(jj * C, (jj + 1) * C)
            vh = (v0, v1)[jj]
            if merged:
                o = o2[hr, :]
                kv = kv2[rs, jj * GLA_DV:(jj + 1) * GLA_DV]
            else:
                att = jnp.where(tril2[0:C, 0:C], _dot_nt(qm2[hr, :], km[:, pls]), 0.0).astype(BF)
                o = _dot(jnp.concatenate([qe2[hr, :], att], axis=1), jnp.concatenate([s_pair, vh], axis=0))
                kv = _dot(klt[rs, :], vh)
            state[hh] = state[hh] * dcol[rs, :] + kv
            out.append((_rms_norm(o, gn) * _silu(gg[:, hh * GLA_DV:(hh + 1) * GLA_DV])).astype(BF))
    return out


def _gla_kernel(q_ref, k_ref, v_ref, la_ref, gg_ref, s0_ref, gn_ref, og_ref, sfin_ref, s_scr, *, C, n_chunks,
                independent):
    NB = q_ref.shape[0]
    if not independent:
        i = pl.program_id(1)

        @pl.when(i == 0)
        def _():
            for nb in range(NB):
                s_scr[nb] = s0_ref[0]

        states = [[s_scr[nb, hh] for hh in range(GLA_HEADS)] for nb in range(NB)]

    consts = _gla_consts(C)
    gn = gn_ref[...]
    b_all = [_gla_cumsum(la_ref[nb], C) for nb in range(NB)]

    for c in range(n_chunks):
        r = slice(c * C, (c + 1) * C)
        for nb in range(NB):
            state = [s0_ref[c, hh] for hh in range(GLA_HEADS)] if independent else states[nb]
            og = _gla_chunk(b_all[nb][r, :], q_ref[nb, r, :], k_ref[nb, r, :], v_ref[nb, r, :], gg_ref[nb, r, :],
                            state, gn, consts)
            for hh in range(GLA_HEADS):
                og_ref[nb, r, hh * GLA_DV:(hh + 1) * GLA_DV] = og[hh]
                if independent:
                    sfin_ref[c, hh] = state[hh]

    if not independent:
        for nb in range(NB):
            for hh in range(GLA_HEADS):
                s_scr[nb, hh] = states[nb][hh]

        @pl.when(i == pl.num_programs(1) - 1)
        def _():
            for nb in range(NB):
                for hh in range(GLA_HEADS):
                    sfin_ref[nb, hh] = states[nb][hh]


def _gla(q, k, v, la, gg, s0, gn, *, B, L, C, tg, rows_per_step=1, independent=False):
    nt = L // tg
    n_chunks = tg // C
    NB = rows_per_step
    blk = lambda b, i: (b, i, 0)
    st = (GLA_HEADS, GLA_DK, GLA_DV)
    if independent:
        assert nt == 1 and NB == 1
        s0_spec = pl.BlockSpec((n_chunks,) + st, lambda b, i: (b, 0, 0, 0))
        sfin_spec = s0_spec
        n_states = B * n_chunks
    else:
        s0_spec = pl.BlockSpec((1,) + st, lambda b, i: (0, 0, 0, 0))
        sfin_spec = pl.BlockSpec((NB,) + st, lambda b, i: (b, 0, 0, 0))
        n_states = B
    in_specs = [pl.BlockSpec((NB, tg, 256), blk), pl.BlockSpec((NB, tg, 256), blk), pl.BlockSpec((NB, tg, 512), blk),
                pl.BlockSpec((NB, tg, 256), blk), pl.BlockSpec((NB, tg, 512), blk),
                s0_spec, pl.BlockSpec((1, GLA_DV), lambda b, i: (0, 0))]
    out_shape = [jax.ShapeDtypeStruct((B, L, 512), BF), jax.ShapeDtypeStruct((n_states,) + st, F32)]
    out_specs = [pl.BlockSpec((NB, tg, 512), blk), sfin_spec]
    r3 = lambda a: a.reshape(B, L, a.shape[-1])
    og, sfin = pl.pallas_call(
        functools.partial(_gla_kernel, C=C, n_chunks=n_chunks, independent=independent),
        out_shape=out_shape, grid=(B // NB, nt), in_specs=in_specs, out_specs=out_specs,
        scratch_shapes=[pltpu.VMEM((NB,) + st, F32)],
        compiler_params=pltpu.CompilerParams(dimension_semantics=("parallel", "arbitrary"),
                                             vmem_limit_bytes=VMEM_LIMIT),
        name="gla_c%d%s" % (C, "_ind" if independent else ""),
    )(r3(q), r3(k), r3(v), r3(la), r3(gg), s0, gn)
    return og.reshape(B * L, 512), sfin


ATTN_TQ = 256
ATTN_SEQ_PER_STEP = 1


def _lanes(x, n):
    if n % 128 == 0:
        return x if n == 128 else jnp.concatenate([x] * (n // 128), axis=1)
    return x[:, 0:n]


def _attn_kernel(qlat_ref, qr_ref, kcat_ref, *rest, tq, has_meta):
    if has_meta:
        kmeta_ref, o_ref, q_scr, s_scr, v_scr, m_scr, l_scr, acc_scr = rest
    else:
        o_ref, q_scr, s_scr, v_scr, m_scr, l_scr, acc_scr = rest
    i = pl.program_id(1)
    NB = qlat_ref.shape[0]
    R = MLA_HEADS * tq
    nk = s_scr.shape[2]
    lane_head = lax.broadcasted_iota(jnp.int32, (tq, 256), 1) // MLA_ROPE
    for nb in range(NB):
        qr = qr_ref[nb]
        for hh in range(MLA_HEADS):
            rows = slice(nb * R + hh * tq, nb * R + (hh + 1) * tq)
            q_scr[rows, 0:256] = qlat_ref[nb, :, hh * 256:(hh + 1) * 256]
            q_scr[rows, 256:512] = jnp.where(lane_head == hh, qr, jnp.zeros_like(qr))
    m_scr[2] = jnp.full(m_scr.shape[1:], NEG_INF, F32)
    l_scr[...] = jnp.zeros(l_scr.shape, F32)
    acc_scr[...] = jnp.zeros(acc_scr.shape, F32)

    def score(kcs, t, mask):
        slot = t % 2
        for nb in range(NB):
            kc = kcs(nb)
            rows = slice(nb * R, (nb + 1) * R)
            v_scr[slot, nb] = kc[:, 0:256]
            s = _dot_nt(q_scr[rows, :], kc)
            key = lax.broadcasted_iota(jnp.int32, (R, nk), 1)
            if mask == "causal":
                tok = lax.broadcasted_iota(jnp.int32, (R, nk), 0) % tq
                s = jnp.where(key <= tok, s, NEG_INF)
            elif mask == "meta":
                s = jnp.where(key < N_META, s, NEG_INF)
            s_scr[slot, rows, :] = s
            m_scr[t % 3, rows, :] = jnp.maximum(m_scr[(t + 2) % 3, rows, :], jnp.max(s, axis=1, keepdims=True))

    def update(t):
        slot = t % 2
        for nb in range(NB):
            rows = slice(nb * R, (nb + 1) * R)
            m_new = m_scr[t % 3, rows, :]
            alpha = jnp.exp2(m_scr[(t + 2) % 3, rows, :] - m_new)
            p = jnp.exp2(s_scr[slot, rows, :] - _lanes(m_new, nk))
            psum = p if nk < 128 else sum(p[:, c * 128:(c + 1) * 128] for c in range(nk // 128))
            l_scr[rows, 0:psum.shape[1]] = alpha[:, 0:psum.shape[1]] * l_scr[rows, 0:psum.shape[1]] + psum
            acc_scr[rows, :] = _lanes(alpha, 256) * acc_scr[rows, :] + _dot(p.astype(BF), v_scr[slot, nb])

    def keys(blk):
        return lambda nb: kcat_ref[nb, pl.ds(pl.multiple_of(blk * tq, tq), tq), :]

    score(keys(i), 0, "causal")
    if has_meta:
        score(lambda nb: kmeta_ref[...], 1, "meta")
        update(0)

        def full(j, carry):
            update(j + 1)
            score(keys(j), j + 2, None)
            return carry

        lax.fori_loop(0, i, full, 0)
        update(i + 1)
    else:
        update(0)
    for nb in range(NB):
        for hh in range(MLA_HEADS):
            rows = slice(nb * R + hh * tq, nb * R + (hh + 1) * tq)
            o = acc_scr[rows, :] * (1.0 / jnp.sum(l_scr[rows, :], axis=1, keepdims=True))
            o_ref[nb, :, hh * 256:(hh + 1) * 256] = o.astype(BF)


def _attn(qlat, qr, kcat, kmeta, *, B, L, tq, seq_per_step=1):
    nq = L // tq
    NB = seq_per_step
    blk = lambda b, i: (b, i, 0)
    in_specs = [pl.BlockSpec((NB, tq, 2048), blk), pl.BlockSpec((NB, tq, 256), blk),
                pl.BlockSpec((NB, L, 512), lambda b, i: (b, 0, 0))]
    args = [qlat.reshape(B, L, 2048), qr.reshape(B, L, 256), kcat.reshape(B, L, 512)]
    if kmeta is not None:
        in_specs.append(pl.BlockSpec((tq, 512), lambda b, i: (0, 0)))
        args.append(jnp.pad(kmeta, ((0, tq - N_META), (0, 0))))
    else:
        assert nq == 1
    R = NB * MLA_HEADS * tq
    out = pl.pallas_call(
        functools.partial(_attn_kernel, tq=tq, has_meta=kmeta is not None),
        out_shape=jax.ShapeDtypeStruct((B, L, 2048), BF), grid=(B // NB, nq), in_specs=in_specs,
        out_specs=pl.BlockSpec((NB, tq, 2048), blk),
        scratch_shapes=[pltpu.VMEM((R, 512), BF), pltpu.VMEM((2, R, tq), F32), pltpu.VMEM((2, NB, tq, 256), BF),
                        pltpu.VMEM((3, R, 128), F32), pltpu.VMEM((R, 128), F32), pltpu.VMEM((R, 256), F32)],
        compiler_params=pltpu.CompilerParams(dimension_semantics=("parallel", "arbitrary"),
                                             vmem_limit_bytes=VMEM_LIMIT),
        name="attn_t%d" % tq,
    )(*args)
    return out.reshape(B * L, 2048)


NEW_PAD = 16
KEY_CH = 1024
KEY_GROUP = 8192
PATTN_SEQ = 2


def _pattn_kernel(pt_ref, ql_ref, qr_ref, cn_ref, kn_ref, cckv_hbm, ckr_hbm, o_ref, ckv_buf, kr_buf, kb_scr,
                  sem, *, n_pages, t_new):
    g = pl.program_id(0)
    ng = pl.num_programs(0)
    slot = g % 2
    n_keys = n_pages * PAGE

    def copies(step, sl):
        out = []
        for j in range(PATTN_SEQ):
            for pg in range(n_pages):
                page = pt_ref[step * PATTN_SEQ + j, pg]
                out.append(pltpu.make_async_copy(
                    cckv_hbm.at[page], ckv_buf.at[sl * PATTN_SEQ + j, pl.ds(pg * PAGE, PAGE), :], sem.at[0, sl]))
                out.append(pltpu.make_async_copy(
                    ckr_hbm.at[page], kr_buf.at[sl * PATTN_SEQ + j, :, pl.ds(pg * PAGE, PAGE)], sem.at[1, sl]))
        return out

    @pl.when(g == 0)
    def _():
        for c in copies(0, 0):
            c.start()

    for c in copies(jnp.minimum(g + 1, ng - 1), 1 - slot):
        c.start()
    for c in copies(g, slot):
        c.wait()

    for j in range(PATTN_SEQ):
        buf = slot * PATTN_SEQ + j
        ql = ql_ref[j]
        qr = qr_ref[j]
        R = ql.shape[0]
        cn = cn_ref[j].astype(BF)
        kn = kn_ref[j].astype(BF)
        s_new = _dot_nt(ql, cn) + _dot_nt(qr, kn)
        tok = lax.broadcasted_iota(jnp.int32, (R, NEW_PAD), 0) % t_new
        key = lax.broadcasted_iota(jnp.int32, (R, NEW_PAD), 1)
        s_new = jnp.where(key <= tok, s_new, NEG_INF)
        m_new = jnp.max(s_new, axis=1, keepdims=True)
        p_new = jnp.exp2(s_new - m_new)
        parts = [(m_new, jnp.sum(p_new, axis=1, keepdims=True), _dot(p_new.astype(BF), cn))]
        for grp in range(n_keys // KEY_GROUP):
            gs = slice(grp * KEY_GROUP, (grp + 1) * KEY_GROUP)
            for ch in range(grp * (KEY_GROUP // KEY_CH), (grp + 1) * (KEY_GROUP // KEY_CH)):
                ks = slice(ch * KEY_CH, (ch + 1) * KEY_CH)
                kb_scr[j, ks, :] = ckv_buf[buf, ks, :].astype(BF)
            s_g = _dot_nt(ql, kb_scr[j, gs, :]) + _dot(qr, kr_buf[buf, :, gs].astype(BF))
            m_g = jnp.max(s_g, axis=1, keepdims=True)
            p_g = jnp.exp2(s_g - m_g)
            parts.append((m_g, jnp.sum(p_g, axis=1, keepdims=True), _dot(p_g.astype(BF), kb_scr[j, gs, :])))
        m = functools.reduce(jnp.maximum, [pt[0] for pt in parts])
        l = sum(pt[1] * jnp.exp2(pt[0] - m) for pt in parts)
        acc = sum(pt[2] * jnp.exp2(pt[0] - m) for pt in parts)
        o_ref[j] = acc * (1.0 / l)

    @pl.when(g == ng - 1)
    def _():
        for c in copies(g, 1 - slot):
            c.wait()


def _pattn(page_table, ql, qr, cn, kn, cache_ckv, cache_krope_t, *, t_new):
    Bd, n_pages = page_table.shape
    R = ql.shape[1]
    n_keys = n_pages * PAGE
    G = PATTN_SEQ
    blk = lambda b, pt: (b, 0, 0)
    grid_spec = pltpu.PrefetchScalarGridSpec(
        num_scalar_prefetch=1, grid=(Bd // G,),
        in_specs=[pl.BlockSpec((G, R, MLA_KV_LORA), blk), pl.BlockSpec((G, R, MLA_ROPE), blk),
                  pl.BlockSpec((G, NEW_PAD, MLA_KV_LORA), blk), pl.BlockSpec((G, NEW_PAD, MLA_ROPE), blk),
                  pl.BlockSpec(memory_space=pl.ANY), pl.BlockSpec(memory_space=pl.ANY)],
        out_specs=pl.BlockSpec((G, R, MLA_KV_LORA), blk),
        scratch_shapes=[pltpu.VMEM((2 * G, n_keys, MLA_KV_LORA), F32), pltpu.VMEM((2 * G, MLA_ROPE, n_keys), F32),
                        pltpu.VMEM((G, n_keys, MLA_KV_LORA), BF), pltpu.SemaphoreType.DMA((2, 2))])
    return pl.pallas_call(
        functools.partial(_pattn_kernel, n_pages=n_pages, t_new=t_new),
        out_shape=jax.ShapeDtypeStruct((Bd, R, MLA_KV_LORA), F32), grid_spec=grid_spec,
        compiler_params=pltpu.CompilerParams(dimension_semantics=("arbitrary",), vmem_limit_bytes=VMEM_LIMIT),
        name="pattn",
    )(page_table, ql, qr, cn, kn, cache_ckv, cache_krope_t)


FF_CH = 256
N_FF_CH = D_FF // FF_CH


def _ffn_kernel(x_ref, og_ref, ol_ref, lng_ref, lnb_ref, wuv_ref, wo_ref, l1g_ref, l1b_ref, wf_ref,
                cw_ref, cb_ref, wd_ref, l2g_ref, l2b_ref, *rest, tm, inject):
    if inject:
        inj1_ref, inj2_ref, m1_ref, m2_ref, y_ref, a_ref, act_scr = rest
    else:
        halo0_ref, y_ref, alast_ref, act_scr, halo_scr = rest
        i = pl.program_id(1)

        @pl.when(i == 0)
        def _():
            halo_scr[...] = halo0_ref[...]

    h = _layer_norm(x_ref[...], lng_ref[...], lnb_ref[...])
    om = jnp.concatenate([_dot(ol_ref[:, j * 512:(j + 1) * 512], wuv_ref[j]) for j in range(MLA_HEADS // 2)],
                         axis=1)
    mix = _dot(og_ref[...], wo_ref[0:512, :]) + _dot(om.astype(BF), wo_ref[512:1024, :])
    h1 = _layer_norm(DN_ALPHA * h + mix, l1g_ref[...], l1b_ref[...])
    h1b = h1.astype(BF)
    ridx = lax.broadcasted_iota(jnp.int32, (tm, FF_CH), 0)
    for c in range(N_FF_CH):
        cs = slice(c * FF_CH, (c + 1) * FF_CH)
        a = _dot(h1b, wf_ref[:, cs])
        up = _dot(h1b, wf_ref[:, D_FF + c * FF_CH:D_FF + (c + 1) * FF_CH])
        r1 = pltpu.roll(a, 1, 0)
        r2 = pltpu.roll(a, 2, 0)
        if inject:
            a1 = jnp.where(m1_ref[...] > 0.0, inj1_ref[:, cs], r1)
            a2 = jnp.where(m2_ref[...] > 0.0, inj2_ref[:, cs], r2)
            a_ref[:, cs] = a
        else:
            hl = halo_scr[:, cs]
            a1 = jnp.where(ridx == 0, hl[1:2, :], r1)
            a2 = jnp.where(ridx == 0, hl[0:1, :], jnp.where(ridx == 1, hl[1:2, :], r2))
            halo_scr[:, cs] = a[tm - 2:tm, :]
            alast_ref[0, :, cs] = a[tm - 2:tm, :]
        conv = cb_ref[:, cs] + cw_ref[0:1, cs] * a2 + cw_ref[1:2, cs] * a1 + cw_ref[2:3, cs] * a
        act_scr[:, cs] = (_silu(conv) * up).astype(BF)
    ffn = _dot(act_scr[...], wd_ref[...])
    y_ref[...] = _layer_norm(DN_ALPHA * h1 + ffn, l2g_ref[...], l2b_ref[...])


def _ffn(x, og, ol, w, *, B, L, tm, halo0=None, inject=None):
    nt = L // tm
    T = B * L
    row = lambda b, i: (b * nt + i, 0)
    c2 = lambda b, i: (0, 0)
    c3 = lambda b, i: (0, 0, 0)
    once = pl.Buffered(1)
    in_specs = [pl.BlockSpec((tm, D_MODEL), row), pl.BlockSpec((tm, 512), row), pl.BlockSpec((tm, 2048), row),
                pl.BlockSpec((1, D_MODEL), c2), pl.BlockSpec((1, D_MODEL), c2),
                pl.BlockSpec((MLA_HEADS // 2, 512, 128), c3, pipeline_mode=once),
                pl.BlockSpec((D_MODEL, D_MODEL), c2, pipeline_mode=once),
                pl.BlockSpec((1, D_MODEL), c2), pl.BlockSpec((1, D_MODEL), c2),
                pl.BlockSpec((D_MODEL, 2 * D_FF), c2, pipeline_mode=once),
                pl.BlockSpec((3, D_FF), c2), pl.BlockSpec((1, D_FF), c2),
                pl.BlockSpec((D_FF, D_MODEL), c2, pipeline_mode=once),
                pl.BlockSpec((1, D_MODEL), c2), pl.BlockSpec((1, D_MODEL), c2)]
    args = [x, og, ol, w["lng"], w["lnb"], w["wuv"], w["wo"], w["l1g"], w["l1b"], w["wf"],
            w["cw"], w["cb"], w["wd"], w["l2g"], w["l2b"]]
    scratch = [pltpu.VMEM((tm, D_FF), BF)]
    if inject is not None:
        inj1, inj2, m1, m2 = inject
        in_specs += [pl.BlockSpec((tm, D_FF), row), pl.BlockSpec((tm, D_FF), row),
                     pl.BlockSpec((tm, 1), row), pl.BlockSpec((tm, 1), row)]
        args += [inj1, inj2, m1, m2]
        out_shape = [jax.ShapeDtypeStruct((T, D_MODEL), F32), jax.ShapeDtypeStruct((T, D_FF), F32)]
        out_specs = [pl.BlockSpec((tm, D_MODEL), row), pl.BlockSpec((tm, D_FF), row)]
    else:
        in_specs.append(pl.BlockSpec((2, D_FF), c2))
        args.append(halo0)
        out_shape = [jax.ShapeDtypeStruct((T, D_MODEL), F32), jax.ShapeDtypeStruct((B, 2, D_FF), F32)]
        out_specs = [pl.BlockSpec((tm, D_MODEL), row), pl.BlockSpec((1, 2, D_FF), lambda b, i: (b, 0, 0))]
        scratch.append(pltpu.VMEM((2, D_FF), F32))
    return pl.pallas_call(
        functools.partial(_ffn_kernel, tm=tm, inject=inject is not None),
        out_shape=out_shape, grid=(B, nt), in_specs=in_specs, out_specs=out_specs, scratch_shapes=scratch,
        compiler_params=pltpu.CompilerParams(dimension_semantics=("parallel", "arbitrary"),
                                             vmem_limit_bytes=VMEM_LIMIT),
        name="ffn_inject" if inject is not None else "ffn_seq",
    )(*args)


def _rope_tables(pos):
    half = MLA_ROPE // 2
    inv = ROPE_BASE ** (-jnp.arange(half, dtype=F32) / half)
    ang = pos.astype(F32)[:, None] * inv[None, :]
    cos = jnp.concatenate([jnp.cos(ang)] * 2, axis=1)
    sin = jnp.concatenate([jnp.sin(ang)] * 2, axis=1)
    pad = jnp.zeros((pos.shape[0], 128 - MLA_ROPE), F32)
    return (jnp.tile(cos, (1, MLA_HEADS)), jnp.tile(sin, (1, MLA_HEADS)),
            jnp.concatenate([cos, pad], axis=1), jnp.concatenate([sin, pad], axis=1))


def _swap_halves(w):
    half = w.shape[-1] // 2
    return jnp.concatenate([-w[..., half:], w[..., :half]], axis=-1)


def _prep_weights(ln_emb_g, ln_emb_b, w_in, w_gate_up, b_gate, gla_norm_g, mla_q_norm_g, mla_kv_norm_g,
                  w_uq, w_uk, w_uv, w_o, ln1_g, ln1_b, w_ffn_in, conv_w, conv_b, w_down, ln2_g, ln2_b):
    o_alr, o_cq, o_ckv, o_kr = 1536, 1552, 1936, 2192
    w_kr = w_in[:, o_kr:o_kr + MLA_ROPE]
    last = jnp.concatenate([w_kr, _swap_halves(w_kr), w_in[:, o_alr:o_cq],
                            jnp.zeros((D_MODEL, 128 - 2 * MLA_ROPE - GLA_RANK), F32)], axis=1)
    win = jnp.concatenate([w_in[:, :o_alr], w_in[:, o_cq:o_kr], last], axis=1).astype(BF)
    wg = jnp.zeros((128, 256), F32).at[LAST_ALR:LAST_ALR + GLA_RANK].set(w_gate_up).astype(BF)
    uq = w_uq.reshape(MLA_Q_LORA, MLA_HEADS, MLA_NOPE + MLA_ROPE)
    uq_r = uq[:, :, MLA_NOPE:]
    wuq = jnp.concatenate([uq[:, :, :MLA_NOPE].reshape(MLA_Q_LORA, 512), uq_r.reshape(MLA_Q_LORA, 256),
                           _swap_halves(uq_r).reshape(MLA_Q_LORA, 256)], axis=1).astype(BF)
    ukt = jnp.transpose(w_uk.reshape(MLA_KV_LORA, MLA_HEADS // 2, 2, MLA_NOPE), (1, 2, 3, 0))
    z64 = jnp.zeros_like(ukt[:, 0])
    wuk = jnp.concatenate([jnp.concatenate([ukt[:, 0], z64], axis=2),
                           jnp.concatenate([z64, ukt[:, 1]], axis=2)], axis=1).astype(BF)
    uv = jnp.transpose(w_uv.reshape(MLA_KV_LORA, MLA_HEADS, MLA_DV), (1, 0, 2))
    uv = uv.reshape(MLA_HEADS // 2, 2, MLA_KV_LORA, MLA_DV)
    zz = jnp.zeros_like(uv[:, 0])
    wuv = jnp.concatenate([jnp.concatenate([uv[:, 0], zz], axis=2), jnp.concatenate([zz, uv[:, 1]], axis=2)],
                          axis=1).astype(BF)
    r = lambda a: a.reshape(1, -1)
    return dict(lng=r(ln_emb_g), lnb=r(ln_emb_b), win=win, wg=wg, bg=r(b_gate), qng=r(mla_q_norm_g),
                kvg=r(mla_kv_norm_g), wuq=wuq, wuk=wuk, wuv=wuv, wo=w_o.astype(BF), l1g=r(ln1_g), l1b=r(ln1_b),
                wf=w_ffn_in.astype(BF), cw=conv_w, cb=r(conv_b),
                wd=w_down.astype(BF), l2g=r(ln2_g), l2b=r(ln2_b), gn=r(gla_norm_g))


def kernel(x_prompt, x_sample, cache_ckv, cache_krope, page_table, state_gla, state_conv, meta_tokens, ln_emb_g, ln_emb_b, w_in, w_gate_up, b_gate, gla_norm_g, mla_q_norm_g, mla_kv_norm_g, w_uq, w_uk, w_uv, w_o, ln1_g, ln1_b, w_ffn_in, conv_w, conv_b, w_down, ln2_g, ln2_b):
    B, S, _ = x_prompt.shape
    Bd, Td, _ = x_sample.shape
    past_len = page_table.shape[1] * PAGE
    n_small = N_META + Bd * Td
    w = _prep_weights(ln_emb_g, ln_emb_b, w_in, w_gate_up, b_gate, gla_norm_g, mla_q_norm_g, mla_kv_norm_g,
                      w_uq, w_uk, w_uv, w_o, ln1_g, ln1_b, w_ffn_in, conv_w, conv_b, w_down, ln2_g, ln2_b)

    xs = jnp.concatenate([meta_tokens.astype(F32), x_sample.reshape(Bd * Td, D_MODEL)], axis=0)
    pos_s = jnp.concatenate([jnp.arange(N_META, dtype=jnp.int32),
                             jnp.tile(past_len + jnp.arange(Td, dtype=jnp.int32), Bd)])
    (qg_s, kg_s, vg_s, la_s, gg_s, qlat_s, qr_s, ckv_s, kr_s, kcat_s) = _proj(
        xs, w, _rope_tables(pos_s), B=1, L=n_small, tm=n_small)

    M = N_META
    zero_state = jnp.zeros((1, GLA_HEADS, GLA_DK, GLA_DV), F32)
    og_m, s_meta = _gla(qg_s[:M], kg_s[:M], vg_s[:M], la_s[:M], gg_s[:M], zero_state, w["gn"],
                        B=1, L=M, C=M, tg=M)
    ol_m = _attn(qlat_s[:M], qr_s[:M], kcat_s[:M], None, B=1, L=M, tq=M)

    TP = 16
    padt = lambda a: jnp.pad(a[M:].reshape(Bd, Td, -1), ((0, 0), (0, TP - Td), (0, 0))).reshape(Bd * TP, -1)
    og_sp, gla_sample = _gla(padt(qg_s), padt(kg_s), padt(vg_s), padt(la_s), padt(gg_s), state_gla, w["gn"],
                             B=Bd // GLA_SEQ_PER_STEP, L=GLA_SEQ_PER_STEP * TP, C=TP, tg=GLA_SEQ_PER_STEP * TP,
                             independent=True)
    og_s = og_sp.reshape(Bd, TP, 512)[:, :Td].reshape(Bd * Td, 512)

    heads_first = lambda a, d: jnp.transpose(a[M:].reshape(Bd, Td, MLA_HEADS, d), (0, 2, 1, 3)).reshape(
        Bd, MLA_HEADS * Td, d)
    padn = lambda a: jnp.pad(a[M:].reshape(Bd, Td, -1), ((0, 0), (0, NEW_PAD - Td), (0, 0)))
    o_s = _pattn(page_table, heads_first(qlat_s, MLA_KV_LORA), heads_first(qr_s, MLA_ROPE),
                 padn(ckv_s), padn(kr_s), cache_ckv, jnp.swapaxes(cache_krope, 1, 2), t_new=Td)
    ol_s = jnp.transpose(o_s.reshape(Bd, MLA_HEADS, Td, MLA_KV_LORA), (0, 2, 1, 3)).reshape(
        Bd * Td, MLA_HEADS * MLA_KV_LORA).astype(BF)

    sc = state_conv.astype(F32)
    zrow = jnp.zeros((Bd, 1, D_FF), F32)
    inj1 = jnp.concatenate([sc[:, 1:2], zrow, zrow, zrow], axis=1).reshape(Bd * Td, D_FF)
    inj2 = jnp.concatenate([sc[:, 0:1], sc[:, 1:2], zrow, zrow], axis=1).reshape(Bd * Td, D_FF)
    zmeta = jnp.zeros((M, D_FF), F32)
    t_in = np.concatenate([np.arange(M), np.tile(np.arange(Td), Bd)])
    m1 = jnp.asarray((t_in == 0).astype(np.float32)[:, None])
    m2 = jnp.asarray((t_in <= 1).astype(np.float32)[:, None])
    y_s, a_s = _ffn(xs, jnp.concatenate([og_m, og_s], axis=0), jnp.concatenate([ol_m, ol_s], axis=0), w,
                    B=1, L=n_small, tm=176,
                    inject=(jnp.concatenate([zmeta, inj1], axis=0), jnp.concatenate([zmeta, inj2], axis=0), m1, m2))

    TM = 512
    xp = x_prompt.reshape(B * S, D_MODEL)
    pos_p = N_META + jnp.arange(S, dtype=jnp.int32)
    (qg_p, kg_p, vg_p, la_p, gg_p, qlat_p, qr_p, ckv_p, kr_p, kcat_p) = _proj(
        xp, w, _rope_tables(pos_p), B=B, L=S, tm=TM)
    og_p, gla_p = _gla(qg_p, kg_p, vg_p, la_p, gg_p, s_meta, w["gn"], B=B, L=S, C=GLA_CHUNK, tg=TM,
                       rows_per_step=GLA_ROWS_PER_STEP)
    ol_p = _attn(qlat_p, qr_p, kcat_p, kcat_s[:M], B=B, L=S, tq=ATTN_TQ, seq_per_step=ATTN_SEQ_PER_STEP)
    y_p, conv_prompt = _ffn(xp, og_p, ol_p, w, B=B, L=S, tm=TM, halo0=a_s[M - 2:M])

    y_prompt = y_p.reshape(B, S, D_MODEL)
    y_sample = y_s[M:].reshape(Bd, Td, D_MODEL)
    ckv_prompt = jnp.concatenate([jnp.broadcast_to(ckv_s[None, :M], (B, M, MLA_KV_LORA)),
                                  ckv_p.reshape(B, S, MLA_KV_LORA)], axis=1)
    krope_prompt = jnp.concatenate([jnp.broadcast_to(kr_s[None, :M], (B, M, MLA_ROPE)),
                                    kr_p.reshape(B, S, MLA_ROPE)], axis=1)
    ckv_sample = ckv_s[M:].reshape(Bd, Td, MLA_KV_LORA)
    krope_sample = kr_s[M:].reshape(Bd, Td, MLA_ROPE)
    conv_sample = a_s[M:].reshape(Bd, Td, D_FF)[:, Td - 2:]
    return (y_prompt, y_sample, ckv_prompt, krope_prompt, gla_p, conv_prompt,
            ckv_sample, krope_sample, gla_sample, conv_sample)
```

```python
import functools

import numpy as np
import jax
import jax.numpy as jnp
from jax import lax
from jax.experimental import pallas as pl
from jax.experimental.pallas import tpu as pltpu

F32 = jnp.float32
BF = jnp.bfloat16

D_MODEL = 1024
N_META = 16
GLA_HEADS = 4
GLA_DK = 64
GLA_DV = 128
GLA_RANK = 16
GLA_TAU = 16.0
GLA_CHUNK = 64
MLA_HEADS = 8
MLA_DV = 64
MLA_NOPE = 64
MLA_ROPE = 32
MLA_Q_LORA = 384
MLA_KV_LORA = 256
MLA_SCALE = (MLA_NOPE + MLA_ROPE) ** -0.5
Q_SCALE = MLA_SCALE * 1.4426950408889634
ROPE_BASE = 10000.0
D_FF = 2816
DN_ALPHA = 2.0 ** 0.25
NORM_EPS = 1e-5
NEG_INF = -1e30
PAGE = 128

C_Q, C_K, C_V, C_G, C_CQ, C_CKV, C_LAST, C_END = 0, 256, 512, 1024, 1536, 1920, 2176, 2304
LAST_ALR = 64

VMEM_LIMIT = 56 * 1024 * 1024


def _dot(a, b):
    return jnp.dot(a, b, preferred_element_type=F32)


def _dot_nt(a, b):
    return lax.dot_general(a, b, (((1,), (1,)), ((), ())), preferred_element_type=F32)


def _dot_tn(a, b):
    return lax.dot_general(a, b, (((0,), (0,)), ((), ())), preferred_element_type=F32)


def _layer_norm(x, g, b):
    mu = jnp.mean(x, axis=-1, keepdims=True)
    xc = x - mu
    var = jnp.mean(xc * xc, axis=-1, keepdims=True)
    return xc * lax.rsqrt(var + NORM_EPS) * g + b


def _rms_norm(x, g):
    ms = jnp.mean(x * x, axis=-1, keepdims=True)
    return x * lax.rsqrt(ms + NORM_EPS) * g


def _silu(x):
    return x * (1.0 / (1.0 + jnp.exp(-x)))


def _proj_kernel(x_ref, lng_ref, lnb_ref, win_ref, wg_ref, bg_ref, qng_ref, kvg_ref, wuq_ref, wuk_ref,
                 cq_ref, sq_ref, ck_ref, sk_ref,
                 qg_ref, kg_ref, vg_ref, la_ref, gg_ref, qlat_ref, qr_ref, ckv_ref, kr_ref, kcat_ref):
    h = _layer_norm(x_ref[...], lng_ref[...], lnb_ref[...])
    z = _dot(h.astype(BF), win_ref[...])
    qg_ref[...] = z[:, C_Q:C_K] * (GLA_DK ** -0.5)
    kg_ref[...] = z[:, C_K:C_V]
    vg_ref[...] = z[:, C_V:C_G].astype(BF)
    gg_ref[...] = z[:, C_G:C_CQ]
    last = z[:, C_LAST:C_END]
    y = _dot(last.astype(BF), wg_ref[...]) + bg_ref[...]
    la_ref[...] = (jnp.minimum(y, 0.0) - jnp.log1p(jnp.exp(-jnp.abs(y)))) * (1.0 / GLA_TAU)
    cqn = _rms_norm(z[:, C_CQ:C_CKV], qng_ref[...])
    qa = _dot(cqn.astype(BF), wuq_ref[...])
    qn = (qa[:, 0:512] * Q_SCALE).astype(BF)
    for pair in range(MLA_HEADS // 2):
        qlat_ref[:, pair * 512:(pair + 1) * 512] = _dot(qn[:, pair * 128:(pair + 1) * 128], wuk_ref[pair]).astype(BF)
    qr = qa[:, 512:768] * cq_ref[...] + qa[:, 768:1024] * sq_ref[...]
    qr_ref[...] = (qr * Q_SCALE).astype(BF)
    ckv = _rms_norm(z[:, C_CKV:C_LAST], kvg_ref[...])
    ckv_ref[...] = ckv
    kr = last * ck_ref[...] + pltpu.roll(last, 96, 1) * sk_ref[...]
    kr_ref[...] = kr[:, 0:MLA_ROPE]
    kr2 = kr + pltpu.roll(kr, 32, 1)
    kr4 = kr2 + pltpu.roll(kr2, 64, 1)
    kcat_ref[:, 0:256] = ckv.astype(BF)
    kcat_ref[:, 256:384] = kr4.astype(BF)
    kcat_ref[:, 384:512] = kr4.astype(BF)


def _proj(x, w, tabs, *, B, L, tm):
    T = B * L
    nt = L // tm
    cq, sq, ck, sk = tabs
    row = lambda b, i: (b * nt + i, 0)
    tab = lambda b, i: (i, 0)
    c2 = lambda b, i: (0, 0)
    c3 = lambda b, i: (0, 0, 0)
    in_specs = [
        pl.BlockSpec((tm, D_MODEL), row),
        pl.BlockSpec((1, D_MODEL), c2), pl.BlockSpec((1, D_MODEL), c2),
        pl.BlockSpec((D_MODEL, C_END), c2),
        pl.BlockSpec((128, 256), c2), pl.BlockSpec((1, 256), c2),
        pl.BlockSpec((1, MLA_Q_LORA), c2), pl.BlockSpec((1, MLA_KV_LORA), c2),
        pl.BlockSpec((MLA_Q_LORA, 1024), c2),
        pl.BlockSpec((MLA_HEADS // 2, 128, 512), c3),
        pl.BlockSpec((tm, 256), tab), pl.BlockSpec((tm, 256), tab),
        pl.BlockSpec((tm, 128), tab), pl.BlockSpec((tm, 128), tab),
    ]
    args = [x, w["lng"], w["lnb"], w["win"], w["wg"], w["bg"], w["qng"], w["kvg"], w["wuq"], w["wuk"], cq, sq, ck, sk]
    outs = [(256, F32), (256, F32), (512, BF), (256, F32), (512, F32), (2048, BF), (256, BF), (256, F32),
            (MLA_ROPE, F32), (512, BF)]
    out_shape = [jax.ShapeDtypeStruct((T, wd), dt) for wd, dt in outs]
    out_specs = [pl.BlockSpec((tm, wd), row) for wd, _ in outs]
    return pl.pallas_call(
        _proj_kernel, out_shape=out_shape, grid=(B, nt), in_specs=in_specs, out_specs=out_specs,
        compiler_params=pltpu.CompilerParams(dimension_semantics=("parallel", "parallel"),
                                             vmem_limit_bytes=VMEM_LIMIT),
        name="proj",
    )(*args)


GLA_SEQ_PER_STEP = 8
GLA_ROWS_PER_STEP = 2
GLA_PAIR_MERGE_ROWS = 32


def _gla_consts(C):
    row = lax.broadcasted_iota(jnp.int32, (2 * C, 2 * C), 0)
    col = lax.broadcasted_iota(jnp.int32, (2 * C, 2 * C), 1)
    tril2 = (col <= row) & (col >= (row // C) * C)
    eye = (lax.broadcasted_iota(jnp.int32, (128, 128), 0) == lax.broadcasted_iota(jnp.int32, (128, 128), 1))
    first_half = ((lax.broadcasted_iota(jnp.int32, (2 * C, 128), 1) < GLA_DK)
                  == (lax.broadcasted_iota(jnp.int32, (2 * C, 128), 0) < C))
    return C, tril2, eye, first_half


def _gla_cumsum(g, C):
    n = g.shape[0]
    row = lax.broadcasted_iota(jnp.int32, (n, n), 0)
    col = lax.broadcasted_iota(jnp.int32, (n, n), 1)
    tri = jnp.where((col <= row) & (col >= (row // C) * C), 1.0, 0.0).astype(BF)
    g1 = g.astype(BF)
    r1 = g - g1.astype(F32)
    g2 = r1.astype(BF)
    g3 = (r1 - g2.astype(F32)).astype(BF)
    return _dot(tri, g1) + _dot(tri, g2) + _dot(tri, g3)


def _gla_chunk(b, q, k, v, gg, state, gn, consts):
    C, tril2, eye, own_lanes = consts
    mid = C // 2
    b_mid = b[mid:mid + 1, :]
    b_last = b[C - 1:C, :]
    qe = (q * jnp.exp(b)).astype(BF)
    qm = (q * jnp.exp(b - b_mid)).astype(BF)
    km = (k * jnp.exp(b_mid - b)).astype(BF)
    kl = k * jnp.exp(b_last - b)
    dec = jnp.exp(b_last)
    out = []
    zeros = jnp.zeros((2 * C, 128), BF)
    for pr in range(GLA_HEADS // 2):
        pls = slice(pr * 128, (pr + 1) * 128)
        v0 = v[:, (2 * pr) * GLA_DV:(2 * pr + 1) * GLA_DV]
        v1 = v[:, (2 * pr + 1) * GLA_DV:(2 * pr + 2) * GLA_DV]
        merged = 2 * C <= GLA_PAIR_MERGE_ROWS
        qm2 = jnp.where(own_lanes, jnp.concatenate([qm[:, pls], qm[:, pls]], axis=0), zeros)
        qe2 = jnp.where(own_lanes, jnp.concatenate([qe[:, pls], qe[:, pls]], axis=0), zeros)
        if merged:
            km2 = jnp.concatenate([km[:, pls], km[:, pls]], axis=0)
            att2 = jnp.where(tril2, _dot_nt(qm2, km2), 0.0).astype(BF)
        s_pair = jnp.concatenate([state[2 * pr], state[2 * pr + 1]], axis=0).astype(BF)
        klt = jnp.transpose(kl[:, pls]).astype(BF)
        dcol = jnp.sum(jnp.where(eye, jnp.broadcast_to(dec[:, pls], (128, 128)), 0.0),
                       axis=1, keepdims=True)
        if merged:
            o2 = _dot(jnp.concatenate([qe2, att2], axis=1), jnp.concatenate([s_pair, v0, v1], axis=0))
            kv2 = _dot(klt, v[:, (2 * pr) * GLA_DV:(2 * pr + 2) * GLA_DV])
        for jj in range(2):
            hh = 2 * pr + jj
            rs = slice(jj * GLA_DK, (jj + 1) * GLA_DK)
            hr = slice(jj * C, (jj + 1) * C)
            vh = (v0, v1)[jj]
            if merged:
                o = o2[hr, :]
                kv = kv2[rs, jj * GLA_DV:(jj + 1) * GLA_DV]
            else:
                att = jnp.where(tril2[0:C, 0:C], _dot_nt(qm2[hr, :], km[:, pls]), 0.0).astype(BF)
                o = _dot(jnp.concatenate([qe2[hr, :], att], axis=1), jnp.concatenate([s_pair, vh], axis=0))
                kv = _dot(klt[rs, :], vh)
            state[hh] = state[hh] * dcol[rs, :] + kv
            out.append((_rms_norm(o, gn) * _silu(gg[:, hh * GLA_DV:(hh + 1) * GLA_DV])).astype(BF))
    return out


def _gla_kernel(q_ref, k_ref, v_ref, la_ref, gg_ref, s0_ref, gn_ref, og_ref, sfin_ref, s_scr, *, C, n_chunks,
                independent):
    NB = q_ref.shape[0]
    if not independent:
        i = pl.program_id(1)

        @pl.when(i == 0)
        def _():
            for nb in range(NB):
                s_scr[nb] = s0_ref[0]

        states = [[s_scr[nb, hh] for hh in range(GLA_HEADS)] for nb in range(NB)]

    consts = _gla_consts(C)
    gn = gn_ref[...]
    b_all = [_gla_cumsum(la_ref[nb], C) for nb in range(NB)]

    for c in range(n_chunks):
        r = slice(c * C, (c + 1) * C)
        for nb in range(NB):
            state = [s0_ref[c, hh] for hh in range(GLA_HEADS)] if independent else states[nb]
            og = _gla_chunk(b_all[nb][r, :], q_ref[nb, r, :], k_ref[nb, r, :], v_ref[nb, r, :], gg_ref[nb, r, :],
                            state, gn, consts)
            for hh in range(GLA_HEADS):
                og_ref[nb, r, hh * GLA_DV:(hh + 1) * GLA_DV] = og[hh]
                if independent:
                    sfin_ref[c, hh] = state[hh]

    if not independent:
        for nb in range(NB):
            for hh in range(GLA_HEADS):
                s_scr[nb, hh] = states[nb][hh]

        @pl.when(i == pl.num_programs(1) - 1)
        def _():
            for nb in range(NB):
                for hh in range(GLA_HEADS):
                    sfin_ref[nb, hh] = states[nb][hh]


def _gla(q, k, v, la, gg, s0, gn, *, B, L, C, tg, rows_per_step=1, independent=False):
    nt = L // tg
    n_chunks = tg // C
    NB = rows_per_step
    blk = lambda b, i: (b, i, 0)
    st = (GLA_HEADS, GLA_DK, GLA_DV)
    if independent:
        assert nt == 1 and NB == 1
        s0_spec = pl.BlockSpec((n_chunks,) + st, lambda b, i: (b, 0, 0, 0))
        sfin_spec = s0_spec
        n_states = B * n_chunks
    else:
        s0_spec = pl.BlockSpec((1,) + st, lambda b, i: (0, 0, 0, 0))
        sfin_spec = pl.BlockSpec((NB,) + st, lambda b, i: (b, 0, 0, 0))
        n_states = B
    in_specs = [pl.BlockSpec((NB, tg, 256), blk), pl.BlockSpec((NB, tg, 256), blk), pl.BlockSpec((NB, tg, 512), blk),
                pl.BlockSpec((NB, tg, 256), blk), pl.BlockSpec((NB, tg, 512), blk),
                s0_spec, pl.BlockSpec((1, GLA_DV), lambda b, i: (0, 0))]
    out_shape = [jax.ShapeDtypeStruct((B, L, 512), BF), jax.ShapeDtypeStruct((n_states,) + st, F32)]
    out_specs = [pl.BlockSpec((NB, tg, 512), blk), sfin_spec]
    r3 = lambda a: a.reshape(B, L, a.shape[-1])
    og, sfin = pl.pallas_call(
        functools.partial(_gla_kernel, C=C, n_chunks=n_chunks, independent=independent),
        out_shape=out_shape, grid=(B // NB, nt), in_specs=in_specs, out_specs=out_specs,
        scratch_shapes=[pltpu.VMEM((NB,) + st, F32)],
        compiler_params=pltpu.CompilerParams(dimension_semantics=("parallel", "arbitrary"),
                                             vmem_limit_bytes=VMEM_LIMIT),
        name="gla_c%d%s" % (C, "_ind" if independent else ""),
    )(r3(q), r3(k), r3(v), r3(la), r3(gg), s0, gn)
    return og.reshape(B * L, 512), sfin


ATTN_TQ = 256
ATTN_SEQ_PER_STEP = 1


def _lanes(x, n):
    if n % 128 == 0:
        return x if n == 128 else jnp.concatenate([x] * (n // 128), axis=1)
    return x[:, 0:n]


def _attn_kernel(qlat_ref, qr_ref, kcat_ref, wuv_ref, *rest, tq, has_meta):
    if has_meta:
        kmeta_ref, o_ref, q_scr, s_scr, p_scr, v_scr, m_scr, l_scr, acc_scr = rest
    else:
        o_ref, q_scr, s_scr, p_scr, v_scr, m_scr, l_scr, acc_scr = rest
    i = pl.program_id(1)
    NB = qlat_ref.shape[0]
    R = MLA_HEADS * tq
    nk = s_scr.shape[2]
    lane_head = lax.broadcasted_iota(jnp.int32, (tq, 256), 1) // MLA_ROPE
    for nb in range(NB):
        qr = qr_ref[nb]
        for hh in range(MLA_HEADS):
            rows = slice(nb * R + hh * tq, nb * R + (hh + 1) * tq)
            q_scr[rows, 0:256] = qlat_ref[nb, :, hh * 256:(hh + 1) * 256]
            q_scr[rows, 256:512] = jnp.where(lane_head == hh, qr, jnp.zeros_like(qr))
    def score(kcs, t, mask, first=False):
        slot = t % 2
        for nb in range(NB):
            kc = kcs(nb)
            rows = slice(nb * R, (nb + 1) * R)
            v_scr[slot, nb] = kc[:, 0:256]
            s = _dot_nt(q_scr[rows, :], kc)
            key = lax.broadcasted_iota(jnp.int32, (R, nk), 1)
            if mask == "causal":
                tok = lax.broadcasted_iota(jnp.int32, (R, nk), 0) % tq
                s = jnp.where(key <= tok, s, NEG_INF)
            elif mask == "meta":
                s = jnp.where(key < N_META, s, NEG_INF)
            s_scr[slot, rows, :] = s
            m_blk = jnp.max(s, axis=1, keepdims=True)
            m_scr[t % 3, rows, :] = (jnp.broadcast_to(m_blk, (R, 128)) if first
                                     else jnp.maximum(m_scr[(t + 2) % 3, rows, :], m_blk))

    def update(t, first=False):
        slot = t % 2
        for nb in range(NB):
            rows = slice(nb * R, (nb + 1) * R)
            m_new = m_scr[t % 3, rows, :]
            p = jnp.exp2(s_scr[slot, rows, :] - _lanes(m_new, nk))
            psum = p if nk < 128 else sum(p[:, c * 128:(c + 1) * 128] for c in range(nk // 128))
            w = psum.shape[1]
            p_scr[rows, :] = p.astype(BF)
            pv = _dot(p_scr[rows, :], v_scr[slot, nb])
            if first:
                if w < 128:
                    l_scr[rows, :] = jnp.zeros((R, 128), F32)
                l_scr[rows, 0:w] = psum
                acc_scr[rows, :] = pv
            else:
                alpha = jnp.exp2(m_scr[(t + 2) % 3, rows, :] - m_new)
                l_scr[rows, 0:w] = alpha[:, 0:w] * l_scr[rows, 0:w] + psum
                acc_scr[rows, :] = _lanes(alpha, 256) * acc_scr[rows, :] + pv

    def keys(blk):
        return lambda nb: kcat_ref[nb, pl.ds(pl.multiple_of(blk * tq, tq), tq), :]

    score(keys(i), 0, "causal", first=True)
    if has_meta:
        score(lambda nb: kmeta_ref[...], 1, "meta")
        update(0, first=True)

        def full(j, carry):
            update(j + 1)
            score(keys(j), j + 2, None)
            return carry

        lax.fori_loop(0, i, full, 0)
        update(i + 1)
    else:
        update(0, first=True)
    for nb in range(NB):
        for pair in range(MLA_HEADS // 2):
            o_pair = []
            for hh in (2 * pair, 2 * pair + 1):
                rows = slice(nb * R + hh * tq, nb * R + (hh + 1) * tq)
                o = acc_scr[rows, :] * (1.0 / jnp.sum(l_scr[rows, :], axis=1, keepdims=True))
                o_pair.append(o.astype(BF))
            o_ref[nb, :, pair * 128:(pair + 1) * 128] = _dot(jnp.concatenate(o_pair, axis=1),
                                                             wuv_ref[pair]).astype(BF)


def _attn(qlat, qr, kcat, kmeta, wuv, *, B, L, tq, seq_per_step=1):
    nq = L // tq
    NB = seq_per_step
    blk = lambda b, i: (b, i, 0)
    in_specs = [pl.BlockSpec((NB, tq, 2048), blk), pl.BlockSpec((NB, tq, 256), blk),
                pl.BlockSpec((NB, L, 512), lambda b, i: (b, 0, 0)),
                pl.BlockSpec((MLA_HEADS // 2, 512, 128), lambda b, i: (0, 0, 0))]
    args = [qlat.reshape(B, L, 2048), qr.reshape(B, L, 256), kcat.reshape(B, L, 512), wuv]
    if kmeta is not None:
        in_specs.append(pl.BlockSpec((tq, 512), lambda b, i: (0, 0)))
        args.append(jnp.pad(kmeta, ((0, tq - N_META), (0, 0))))
    else:
        assert nq == 1
    R = NB * MLA_HEADS * tq
    out = pl.pallas_call(
        functools.partial(_attn_kernel, tq=tq, has_meta=kmeta is not None),
        out_shape=jax.ShapeDtypeStruct((B, L, 512), BF), grid=(B // NB, nq), in_specs=in_specs,
        out_specs=pl.BlockSpec((NB, tq, 512), blk),
        scratch_shapes=[pltpu.VMEM((R, 512), BF), pltpu.VMEM((2, R, tq), F32), pltpu.VMEM((R, tq), BF),
                        pltpu.VMEM((2, NB, tq, 256), BF),
                        pltpu.VMEM((3, R, 128), F32), pltpu.VMEM((R, 128), F32), pltpu.VMEM((R, 256), F32)],
        compiler_params=pltpu.CompilerParams(dimension_semantics=("parallel", "arbitrary"),
                                             vmem_limit_bytes=VMEM_LIMIT),
        name="attn_t%d" % tq,
    )(*args)
    return out.reshape(B * L, 512)


def _uv_kernel(ol_ref, wuv_ref, om_ref):
    for pair in range(MLA_HEADS // 2):
        om_ref[:, pair * 128:(pair + 1) * 128] = _dot(ol_ref[:, pair * 512:(pair + 1) * 512],
                                                      wuv_ref[pair]).astype(BF)


def _uv(ol, wuv):
    T = ol.shape[0]
    return pl.pallas_call(
        _uv_kernel, out_shape=jax.ShapeDtypeStruct((T, 512), BF), grid=(1,),
        in_specs=[pl.BlockSpec((T, 2048), lambda i: (0, 0)),
                  pl.BlockSpec((MLA_HEADS // 2, 512, 128), lambda i: (0, 0, 0))],
        out_specs=pl.BlockSpec((T, 512), lambda i: (0, 0)),
        compiler_params=pltpu.CompilerParams(dimension_semantics=("arbitrary",), vmem_limit_bytes=VMEM_LIMIT),
        name="uv",
    )(ol, wuv)


NEW_PAD = 16
KEY_CH = 1024
KEY_GROUP = 8192
PATTN_SEQ = 2


def _pattn_kernel(pt_ref, ql_ref, qr_ref, cn_ref, kn_ref, cckv_hbm, ckr_hbm, o_ref, ckv_buf, kr_buf, kb_scr,
                  sem, *, n_pages, t_new):
    g = pl.program_id(0)
    ng = pl.num_programs(0)
    slot = g % 2
    n_keys = n_pages * PAGE

    def copies(step, sl):
        out = []
        for j in range(PATTN_SEQ):
            for pg in range(n_pages):
                page = pt_ref[step * PATTN_SEQ + j, pg]
                out.append(pltpu.make_async_copy(
                    cckv_hbm.at[page], ckv_buf.at[sl * PATTN_SEQ + j, pl.ds(pg * PAGE, PAGE), :], sem.at[0, sl]))
                out.append(pltpu.make_async_copy(
                    ckr_hbm.at[page], kr_buf.at[sl * PATTN_SEQ + j, :, pl.ds(pg * PAGE, PAGE)], sem.at[1, sl]))
        return out

    @pl.when(g == 0)
    def _():
        for c in copies(0, 0):
            c.start()

    for c in copies(jnp.minimum(g + 1, ng - 1), 1 - slot):
        c.start()
    for c in copies(g, slot):
        c.wait()

    for j in range(PATTN_SEQ):
        buf = slot * PATTN_SEQ + j
        ql = ql_ref[j]
        qr = qr_ref[j]
        R = ql.shape[0]
        cn = cn_ref[j].astype(BF)
        kn = kn_ref[j].astype(BF)
        s_new = _dot_nt(ql, cn) + _dot_nt(qr, kn)
        tok = lax.broadcasted_iota(jnp.int32, (R, NEW_PAD), 0) % t_new
        key = lax.broadcasted_iota(jnp.int32, (R, NEW_PAD), 1)
        s_new = jnp.where(key <= tok, s_new, NEG_INF)
        m_new = jnp.max(s_new, axis=1, keepdims=True)
        p_new = jnp.exp2(s_new - m_new)
        parts = [(m_new, jnp.sum(p_new, axis=1, keepdims=True), _dot(p_new.astype(BF), cn))]
        for grp in range(n_keys // KEY_GROUP):
            gs = slice(grp * KEY_GROUP, (grp + 1) * KEY_GROUP)
            for ch in range(grp * (KEY_GROUP // KEY_CH), (grp + 1) * (KEY_GROUP // KEY_CH)):
                ks = slice(ch * KEY_CH, (ch + 1) * KEY_CH)
                kb_scr[j, ks, :] = ckv_buf[buf, ks, :].astype(BF)
            s_g = _dot_nt(ql, kb_scr[j, gs, :]) + _dot(qr, kr_buf[buf, :, gs].astype(BF))
            m_g = jnp.max(s_g, axis=1, keepdims=True)
            p_g = jnp.exp2(s_g - m_g)
            parts.append((m_g, jnp.sum(p_g, axis=1, keepdims=True), _dot(p_g.astype(BF), kb_scr[j, gs, :])))
        m = functools.reduce(jnp.maximum, [pt[0] for pt in parts])
        l = sum(pt[1] * jnp.exp2(pt[0] - m) for pt in parts)
        acc = sum(pt[2] * jnp.exp2(pt[0] - m) for pt in parts)
        o_ref[j] = acc * (1.0 / l)

    @pl.when(g == ng - 1)
    def _():
        for c in copies(g, 1 - slot):
            c.wait()


def _pattn(page_table, ql, qr, cn, kn, cache_ckv, cache_krope_t, *, t_new):
    Bd, n_pages = page_table.shape
    R = ql.shape[1]
    n_keys = n_pages * PAGE
    G = PATTN_SEQ
    blk = lambda b, pt: (b, 0, 0)
    grid_spec = pltpu.PrefetchScalarGridSpec(
        num_scalar_prefetch=1, grid=(Bd // G,),
        in_specs=[pl.BlockSpec((G, R, MLA_KV_LORA), blk), pl.BlockSpec((G, R, MLA_ROPE), blk),
                  pl.BlockSpec((G, NEW_PAD, MLA_KV_LORA), blk), pl.BlockSpec((G, NEW_PAD, MLA_ROPE), blk),
                  pl.BlockSpec(memory_space=pl.ANY), pl.BlockSpec(memory_space=pl.ANY)],
        out_specs=pl.BlockSpec((G, R, MLA_KV_LORA), blk),
        scratch_shapes=[pltpu.VMEM((2 * G, n_keys, MLA_KV_LORA), F32), pltpu.VMEM((2 * G, MLA_ROPE, n_keys), F32),
                        pltpu.VMEM((G, n_keys, MLA_KV_LORA), BF), pltpu.SemaphoreType.DMA((2, 2))])
    return pl.pallas_call(
        functools.partial(_pattn_kernel, n_pages=n_pages, t_new=t_new),
        out_shape=jax.ShapeDtypeStruct((Bd, R, MLA_KV_LORA), F32), grid_spec=grid_spec,
        compiler_params=pltpu.CompilerParams(dimension_semantics=("arbitrary",), vmem_limit_bytes=VMEM_LIMIT),
        name="pattn",
    )(page_table, ql, qr, cn, kn, cache_ckv, cache_krope_t)


FFN_TM = 512
FF_CH = 256
N_FF_CH = D_FF // FF_CH


def _ffn_kernel(x_ref, og_ref, om_ref, lng_ref, lnb_ref, wo_ref, l1g_ref, l1b_ref, wf_ref,
                cw_ref, cb_ref, wd_ref, l2g_ref, l2b_ref, *rest, tm, inject):
    if inject:
        inj1_ref, inj2_ref, m1_ref, m2_ref, y_ref, a_ref, act_scr = rest
    else:
        halo0_ref, y_ref, alast_ref, act_scr, halo_scr = rest
        i = pl.program_id(1)

        @pl.when(i == 0)
        def _():
            halo_scr[...] = halo0_ref[...]

    h = _layer_norm(x_ref[...], lng_ref[...], lnb_ref[...])
    mix = _dot(og_ref[...], wo_ref[0:512, :]) + _dot(om_ref[...], wo_ref[512:1024, :])
    h1 = _layer_norm(DN_ALPHA * h + mix, l1g_ref[...], l1b_ref[...])
    h1b = h1.astype(BF)
    ridx = lax.broadcasted_iota(jnp.int32, (tm, FF_CH), 0)
    for c in range(N_FF_CH):
        cs = slice(c * FF_CH, (c + 1) * FF_CH)
        a = _dot(h1b, wf_ref[:, cs])
        up = _dot(h1b, wf_ref[:, D_FF + c * FF_CH:D_FF + (c + 1) * FF_CH])
        r1 = pltpu.roll(a, 1, 0)
        r2 = pltpu.roll(a, 2, 0)
        if inject:
            a1 = jnp.where(m1_ref[...] > 0.0, inj1_ref[:, cs], r1)
            a2 = jnp.where(m2_ref[...] > 0.0, inj2_ref[:, cs], r2)
            a_ref[:, cs] = a
        else:
            hl = halo_scr[:, cs]
            a1 = jnp.where(ridx == 0, hl[1:2, :], r1)
            a2 = jnp.where(ridx == 0, hl[0:1, :], jnp.where(ridx == 1, hl[1:2, :], r2))
            halo_scr[:, cs] = a[tm - 2:tm, :]
            alast_ref[0, :, cs] = a[tm - 2:tm, :]
        conv = cb_ref[:, cs] + cw_ref[0:1, cs] * a2 + cw_ref[1:2, cs] * a1 + cw_ref[2:3, cs] * a
        act_scr[:, cs] = (_silu(conv) * up).astype(BF)
    ffn = _dot(act_scr[...], wd_ref[...])
    y_ref[...] = _layer_norm(DN_ALPHA * h1 + ffn, l2g_ref[...], l2b_ref[...])


def _ffn(x, og, om, w, *, B, L, tm, halo0=None, inject=None):
    nt = L // tm
    T = B * L
    row = lambda b, i: (b * nt + i, 0)
    c2 = lambda b, i: (0, 0)
    once = pl.Buffered(1)
    in_specs = [pl.BlockSpec((tm, D_MODEL), row), pl.BlockSpec((tm, 512), row), pl.BlockSpec((tm, 512), row),
                pl.BlockSpec((1, D_MODEL), c2), pl.BlockSpec((1, D_MODEL), c2),
                pl.BlockSpec((D_MODEL, D_MODEL), c2, pipeline_mode=once),
                pl.BlockSpec((1, D_MODEL), c2), pl.BlockSpec((1, D_MODEL), c2),
                pl.BlockSpec((D_MODEL, 2 * D_FF), c2, pipeline_mode=once),
                pl.BlockSpec((3, D_FF), c2), pl.BlockSpec((1, D_FF), c2),
                pl.BlockSpec((D_FF, D_MODEL), c2, pipeline_mode=once),
                pl.BlockSpec((1, D_MODEL), c2), pl.BlockSpec((1, D_MODEL), c2)]
    args = [x, og, om, w["lng"], w["lnb"], w["wo"], w["l1g"], w["l1b"], w["wf"],
            w["cw"], w["cb"], w["wd"], w["l2g"], w["l2b"]]
    scratch = [pltpu.VMEM((tm, D_FF), BF)]
    if inject is not None:
        inj1, inj2, m1, m2 = inject
        in_specs += [pl.BlockSpec((tm, D_FF), row), pl.BlockSpec((tm, D_FF), row),
                     pl.BlockSpec((tm, 1), row), pl.BlockSpec((tm, 1), row)]
        args += [inj1, inj2, m1, m2]
        out_shape = [jax.ShapeDtypeStruct((T, D_MODEL), F32), jax.ShapeDtypeStruct((T, D_FF), F32)]
        out_specs = [pl.BlockSpec((tm, D_MODEL), row), pl.BlockSpec((tm, D_FF), row)]
    else:
        in_specs.append(pl.BlockSpec((2, D_FF), c2))
        args.append(halo0)
        out_shape = [jax.ShapeDtypeStruct((T, D_MODEL), F32), jax.ShapeDtypeStruct((B, 2, D_FF), F32)]
        out_specs = [pl.BlockSpec((tm, D_MODEL), row), pl.BlockSpec((1, 2, D_FF), lambda b, i: (b, 0, 0))]
        scratch.append(pltpu.VMEM((2, D_FF), F32))
    return pl.pallas_call(
        functools.partial(_ffn_kernel, tm=tm, inject=inject is not None),
        out_shape=out_shape, grid=(B, nt), in_specs=in_specs, out_specs=out_specs, scratch_shapes=scratch,
        compiler_params=pltpu.CompilerParams(dimension_semantics=("parallel", "arbitrary"),
                                             vmem_limit_bytes=VMEM_LIMIT),
        name="ffn_inject" if inject is not None else "ffn_seq",
    )(*args)


def _rope_tables(pos):
    half = MLA_ROPE // 2
    inv = ROPE_BASE ** (-jnp.arange(half, dtype=F32) / half)
    ang = pos.astype(F32)[:, None] * inv[None, :]
    cos = jnp.concatenate([jnp.cos(ang)] * 2, axis=1)
    sin = jnp.concatenate([jnp.sin(ang)] * 2, axis=1)
    pad = jnp.zeros((pos.shape[0], 128 - MLA_ROPE), F32)
    return (jnp.tile(cos, (1, MLA_HEADS)), jnp.tile(sin, (1, MLA_HEADS)),
            jnp.concatenate([cos, pad], axis=1), jnp.concatenate([sin, pad], axis=1))


def _swap_halves(w):
    half = w.shape[-1] // 2
    return jnp.concatenate([-w[..., half:], w[..., :half]], axis=-1)


def _prep_weights(ln_emb_g, ln_emb_b, w_in, w_gate_up, b_gate, gla_norm_g, mla_q_norm_g, mla_kv_norm_g,
                  w_uq, w_uk, w_uv, w_o, ln1_g, ln1_b, w_ffn_in, conv_w, conv_b, w_down, ln2_g, ln2_b):
    o_alr, o_cq, o_ckv, o_kr = 1536, 1552, 1936, 2192
    w_kr = w_in[:, o_kr:o_kr + MLA_ROPE]
    last = jnp.concatenate([w_kr, _swap_halves(w_kr), w_in[:, o_alr:o_cq],
                            jnp.zeros((D_MODEL, 128 - 2 * MLA_ROPE - GLA_RANK), F32)], axis=1)
    win = jnp.concatenate([w_in[:, :o_alr], w_in[:, o_cq:o_kr], last], axis=1).astype(BF)
    wg = jnp.zeros((128, 256), F32).at[LAST_ALR:LAST_ALR + GLA_RANK].set(w_gate_up).astype(BF)
    uq = w_uq.reshape(MLA_Q_LORA, MLA_HEADS, MLA_NOPE + MLA_ROPE)
    uq_r = uq[:, :, MLA_NOPE:]
    wuq = jnp.concatenate([uq[:, :, :MLA_NOPE].reshape(MLA_Q_LORA, 512), uq_r.reshape(MLA_Q_LORA, 256),
                           _swap_halves(uq_r).reshape(MLA_Q_LORA, 256)], axis=1).astype(BF)
    ukt = jnp.transpose(w_uk.reshape(MLA_KV_LORA, MLA_HEADS // 2, 2, MLA_NOPE), (1, 2, 3, 0))
    z64 = jnp.zeros_like(ukt[:, 0])
    wuk = jnp.concatenate([jnp.concatenate([ukt[:, 0], z64], axis=2),
                           jnp.concatenate([z64, ukt[:, 1]], axis=2)], axis=1).astype(BF)
    uv = jnp.transpose(w_uv.reshape(MLA_KV_LORA, MLA_HEADS, MLA_DV), (1, 0, 2))
    uv = uv.reshape(MLA_HEADS // 2, 2, MLA_KV_LORA, MLA_DV)
    zz = jnp.zeros_like(uv[:, 0])
    wuv = jnp.concatenate([jnp.concatenate([uv[:, 0], zz], axis=2), jnp.concatenate([zz, uv[:, 1]], axis=2)],
                          axis=1).astype(BF)
    r = lambda a: a.reshape(1, -1)
    return dict(lng=r(ln_emb_g), lnb=r(ln_emb_b), win=win, wg=wg, bg=r(b_gate), qng=r(mla_q_norm_g),
                kvg=r(mla_kv_norm_g), wuq=wuq, wuk=wuk, wuv=wuv, wo=w_o.astype(BF), l1g=r(ln1_g), l1b=r(ln1_b),
                wf=w_ffn_in.astype(BF), cw=conv_w, cb=r(conv_b),
                wd=w_down.astype(BF), l2g=r(ln2_g), l2b=r(ln2_b), gn=r(gla_norm_g))


def kernel(x_prompt, x_sample, cache_ckv, cache_krope, page_table, state_gla, state_conv, meta_tokens, ln_emb_g, ln_emb_b, w_in, w_gate_up, b_gate, gla_norm_g, mla_q_norm_g, mla_kv_norm_g, w_uq, w_uk, w_uv, w_o, ln1_g, ln1_b, w_ffn_in, conv_w, conv_b, w_down, ln2_g, ln2_b):
    B, S, _ = x_prompt.shape
    Bd, Td, _ = x_sample.shape
    past_len = page_table.shape[1] * PAGE
    n_small = N_META + Bd * Td
    w = _prep_weights(ln_emb_g, ln_emb_b, w_in, w_gate_up, b_gate, gla_norm_g, mla_q_norm_g, mla_kv_norm_g,
                      w_uq, w_uk, w_uv, w_o, ln1_g, ln1_b, w_ffn_in, conv_w, conv_b, w_down, ln2_g, ln2_b)

    xs = jnp.concatenate([meta_tokens.astype(F32), x_sample.reshape(Bd * Td, D_MODEL)], axis=0)
    pos_s = jnp.concatenate([jnp.arange(N_META, dtype=jnp.int32),
                             jnp.tile(past_len + jnp.arange(Td, dtype=jnp.int32), Bd)])
    (qg_s, kg_s, vg_s, la_s, gg_s, qlat_s, qr_s, ckv_s, kr_s, kcat_s) = _proj(
        xs, w, _rope_tables(pos_s), B=1, L=n_small, tm=n_small)

    M = N_META
    zero_state = jnp.zeros((1, GLA_HEADS, GLA_DK, GLA_DV), F32)
    og_m, s_meta = _gla(qg_s[:M], kg_s[:M], vg_s[:M], la_s[:M], gg_s[:M], zero_state, w["gn"],
                        B=1, L=M, C=M, tg=M)
    om_m = _attn(qlat_s[:M], qr_s[:M], kcat_s[:M], None, w["wuv"], B=1, L=M, tq=M)

    TP = 16
    padt = lambda a: jnp.pad(a[M:].reshape(Bd, Td, -1), ((0, 0), (0, TP - Td), (0, 0))).reshape(Bd * TP, -1)
    og_sp, gla_sample = _gla(padt(qg_s), padt(kg_s), padt(vg_s), padt(la_s), padt(gg_s), state_gla, w["gn"],
                             B=Bd // GLA_SEQ_PER_STEP, L=GLA_SEQ_PER_STEP * TP, C=TP, tg=GLA_SEQ_PER_STEP * TP,
                             independent=True)
    og_s = og_sp.reshape(Bd, TP, 512)[:, :Td].reshape(Bd * Td, 512)

    heads_first = lambda a, d: jnp.transpose(a[M:].reshape(Bd, Td, MLA_HEADS, d), (0, 2, 1, 3)).reshape(
        Bd, MLA_HEADS * Td, d)
    padn = lambda a: jnp.pad(a[M:].reshape(Bd, Td, -1), ((0, 0), (0, NEW_PAD - Td), (0, 0)))
    o_s = _pattn(page_table, heads_first(qlat_s, MLA_KV_LORA), heads_first(qr_s, MLA_ROPE),
                 padn(ckv_s), padn(kr_s), cache_ckv, jnp.swapaxes(cache_krope, 1, 2), t_new=Td)
    om_s = _uv(jnp.transpose(o_s.reshape(Bd, MLA_HEADS, Td, MLA_KV_LORA), (0, 2, 1, 3)).reshape(
        Bd * Td, MLA_HEADS * MLA_KV_LORA).astype(BF), w["wuv"])

    sc = state_conv.astype(F32)
    zrow = jnp.zeros((Bd, 1, D_FF), F32)
    inj1 = jnp.concatenate([sc[:, 1:2], zrow, zrow, zrow], axis=1).reshape(Bd * Td, D_FF)
    inj2 = jnp.concatenate([sc[:, 0:1], sc[:, 1:2], zrow, zrow], axis=1).reshape(Bd * Td, D_FF)
    zmeta = jnp.zeros((M, D_FF), F32)
    t_in = np.concatenate([np.arange(M), np.tile(np.arange(Td), Bd)])
    m1 = jnp.asarray((t_in == 0).astype(np.float32)[:, None])
    m2 = jnp.asarray((t_in <= 1).astype(np.float32)[:, None])
    y_s, a_s = _ffn(xs, jnp.concatenate([og_m, og_s], axis=0), jnp.concatenate([om_m, om_s], axis=0), w,
                    B=1, L=n_small, tm=176,
                    inject=(jnp.concatenate([zmeta, inj1], axis=0), jnp.concatenate([zmeta, inj2], axis=0), m1, m2))

    TM = 512
    xp = x_prompt.reshape(B * S, D_MODEL)
    pos_p = N_META + jnp.arange(S, dtype=jnp.int32)
    (qg_p, kg_p, vg_p, la_p, gg_p, qlat_p, qr_p, ckv_p, kr_p, kcat_p) = _proj(
        xp, w, _rope_tables(pos_p), B=B, L=S, tm=TM)
    og_p, gla_p = _gla(qg_p, kg_p, vg_p, la_p, gg_p, s_meta, w["gn"], B=B, L=S, C=GLA_CHUNK, tg=TM,
                       rows_per_step=GLA_ROWS_PER_STEP)
    om_p = _attn(qlat_p, qr_p, kcat_p, kcat_s[:M], w["wuv"], B=B, L=S, tq=ATTN_TQ, seq_per_step=ATTN_SEQ_PER_STEP)
    y_p, conv_prompt = _ffn(xp, og_p, om_p, w, B=B, L=S, tm=FFN_TM, halo0=a_s[M - 2:M])

    y_prompt = y_p.reshape(B, S, D_MODEL)
    y_sample = y_s[M:].reshape(Bd, Td, D_MODEL)
    ckv_prompt = jnp.concatenate([jnp.broadcast_to(ckv_s[None, :M], (B, M, MLA_KV_LORA)),
                                  ckv_p.reshape(B, S, MLA_KV_LORA)], axis=1)
    krope_prompt = jnp.concatenate([jnp.broadcast_to(kr_s[None, :M], (B, M, MLA_ROPE)),
                                    kr_p.reshape(B, S, MLA_ROPE)], axis=1)
    ckv_sample = ckv_s[M:].reshape(Bd, Td, MLA_KV_LORA)
    krope_sample = kr_s[M:].reshape(Bd, Td, MLA_ROPE)
    conv_sample = a_s[M:].reshape(Bd, Td, D_FF)[:, Td - 2:]
    return (y_prompt, y_sample, ckv_prompt, krope_prompt, gla_p, conv_prompt,
            ckv_sample, krope_sample, gla_sample, conv_sample)
```

```python
import functools

import numpy as np
import jax
import jax.numpy as jnp
from jax import lax
from jax.experimental import pallas as pl
from jax.experimental.pallas import tpu as pltpu

F32 = jnp.float32
BF = jnp.bfloat16

D_MODEL = 1024
N_META = 16
GLA_HEADS = 4
GLA_DK = 64
GLA_DV = 128
GLA_RANK = 16
GLA_TAU = 16.0
GLA_CHUNK = 64
MLA_HEADS = 8
MLA_DV = 64
MLA_NOPE = 64
MLA_ROPE = 32
MLA_Q_LORA = 384
MLA_KV_LORA = 256
MLA_SCALE = (MLA_NOPE + MLA_ROPE) ** -0.5
Q_SCALE = MLA_SCALE * 1.4426950408889634
ROPE_BASE = 10000.0
D_FF = 2816
DN_ALPHA = 2.0 ** 0.25
NORM_EPS = 1e-5
NEG_INF = -1e30
PAGE = 128

C_CQ, C_LAST, C_CKV, C_Q, C_K, C_V, C_G, C_END = 0, 384, 512, 768, 1024, 1280, 1792, 2304
LAST_ALR = 64

VMEM_LIMIT = 56 * 1024 * 1024


def _dot(a, b):
    return jnp.dot(a, b, preferred_element_type=F32)


def _dot_nt(a, b):
    return lax.dot_general(a, b, (((1,), (1,)), ((), ())), preferred_element_type=F32)


def _dot_tn(a, b):
    return lax.dot_general(a, b, (((0,), (0,)), ((), ())), preferred_element_type=F32)


def _layer_norm(x, g, b):
    mu = jnp.mean(x, axis=-1, keepdims=True)
    xc = x - mu
    var = jnp.mean(xc * xc, axis=-1, keepdims=True)
    return xc * lax.rsqrt(var + NORM_EPS) * g + b


def _rms_norm(x, g):
    ms = jnp.mean(x * x, axis=-1, keepdims=True)
    return x * lax.rsqrt(ms + NORM_EPS) * g


def _silu(x):
    return x * (1.0 / (1.0 + jnp.exp(-x)))


def _proj_kernel(x_ref, lng_ref, lnb_ref, win_ref, wg_ref, bg_ref, qng_ref, kvg_ref, wuq_ref, wuk_ref,
                 cq_ref, sq_ref, ck_ref, sk_ref,
                 qg_ref, kg_ref, vg_ref, la_ref, gg_ref, qlat_ref, qr_ref, ckv_ref, kr_ref, kcat_ref):
    h = _layer_norm(x_ref[...], lng_ref[...], lnb_ref[...])
    z = _dot(h.astype(BF), win_ref[...])
    qg_ref[...] = z[:, C_Q:C_K] * (GLA_DK ** -0.5)
    kg_ref[...] = z[:, C_K:C_V]
    vg_ref[...] = z[:, C_V:C_G].astype(BF)
    gg_ref[...] = z[:, C_G:C_END]
    last = z[:, C_LAST:C_LAST + 128]
    y = _dot(last.astype(BF), wg_ref[...]) + bg_ref[...]
    la_ref[...] = (jnp.minimum(y, 0.0) - jnp.log1p(jnp.exp(-jnp.abs(y)))) * (1.0 / GLA_TAU)
    cqn = _rms_norm(z[:, C_CQ:C_CQ + MLA_Q_LORA], qng_ref[...])
    qa = _dot(cqn.astype(BF), wuq_ref[...])
    qn = (qa[:, 0:512] * Q_SCALE).astype(BF)
    for pair in range(MLA_HEADS // 2):
        qlat_ref[:, pair * 512:(pair + 1) * 512] = _dot(qn[:, pair * 128:(pair + 1) * 128], wuk_ref[pair]).astype(BF)
    qr = qa[:, 512:768] * cq_ref[...] + qa[:, 768:1024] * sq_ref[...]
    qr_ref[...] = (qr * Q_SCALE).astype(BF)
    ckv = _rms_norm(z[:, C_CKV:C_CKV + MLA_KV_LORA], kvg_ref[...])
    ckv_ref[...] = ckv
    kr = last * ck_ref[...] + pltpu.roll(last, 96, 1) * sk_ref[...]
    kr_ref[...] = kr[:, 0:MLA_ROPE]
    kr2 = kr + pltpu.roll(kr, 32, 1)
    kr4 = kr2 + pltpu.roll(kr2, 64, 1)
    kcat_ref[:, 0:256] = ckv.astype(BF)
    kcat_ref[:, 256:384] = kr4.astype(BF)
    kcat_ref[:, 384:512] = kr4.astype(BF)


def _proj(x, w, tabs, *, B, L, tm):
    T = B * L
    nt = L // tm
    cq, sq, ck, sk = tabs
    row = lambda b, i: (b * nt + i, 0)
    tab = lambda b, i: (i, 0)
    c2 = lambda b, i: (0, 0)
    c3 = lambda b, i: (0, 0, 0)
    in_specs = [
        pl.BlockSpec((tm, D_MODEL), row),
        pl.BlockSpec((1, D_MODEL), c2), pl.BlockSpec((1, D_MODEL), c2),
        pl.BlockSpec((D_MODEL, C_END), c2),
        pl.BlockSpec((128, 256), c2), pl.BlockSpec((1, 256), c2),
        pl.BlockSpec((1, MLA_Q_LORA), c2), pl.BlockSpec((1, MLA_KV_LORA), c2),
        pl.BlockSpec((MLA_Q_LORA, 1024), c2),
        pl.BlockSpec((MLA_HEADS // 2, 128, 512), c3),
        pl.BlockSpec((tm, 256), tab), pl.BlockSpec((tm, 256), tab),
        pl.BlockSpec((tm, 128), tab), pl.BlockSpec((tm, 128), tab),
    ]
    args = [x, w["lng"], w["lnb"], w["win"], w["wg"], w["bg"], w["qng"], w["kvg"], w["wuq"], w["wuk"], cq, sq, ck, sk]
    outs = [(256, F32), (256, F32), (512, BF), (256, F32), (512, F32), (2048, BF), (256, BF), (256, F32),
            (MLA_ROPE, F32), (512, BF)]
    out_shape = [jax.ShapeDtypeStruct((T, wd), dt) for wd, dt in outs]
    out_specs = [pl.BlockSpec((tm, wd), row) for wd, _ in outs]
    return pl.pallas_call(
        _proj_kernel, out_shape=out_shape, grid=(B, nt), in_specs=in_specs, out_specs=out_specs,
        compiler_params=pltpu.CompilerParams(dimension_semantics=("parallel", "parallel"),
                                             vmem_limit_bytes=VMEM_LIMIT),
        name="proj",
    )(*args)


GLA_SEQ_PER_STEP = 8
GLA_ROWS_PER_STEP = 2
GLA_PAIR_MERGE_ROWS = 32


def _gla_consts(C):
    row = lax.broadcasted_iota(jnp.int32, (2 * C, 2 * C), 0)
    col = lax.broadcasted_iota(jnp.int32, (2 * C, 2 * C), 1)
    tril2 = (col <= row) & (col >= (row // C) * C)
    eye = (lax.broadcasted_iota(jnp.int32, (128, 128), 0) == lax.broadcasted_iota(jnp.int32, (128, 128), 1))
    first_half = ((lax.broadcasted_iota(jnp.int32, (2 * C, 128), 1) < GLA_DK)
                  == (lax.broadcasted_iota(jnp.int32, (2 * C, 128), 0) < C))
    return C, tril2, eye, first_half


def _gla_cumsum(g, C):
    n = g.shape[0]
    row = lax.broadcasted_iota(jnp.int32, (n, n), 0)
    col = lax.broadcasted_iota(jnp.int32, (n, n), 1)
    tri = jnp.where((col <= row) & (col >= (row // C) * C), 1.0, 0.0).astype(BF)
    g1 = g.astype(BF)
    r1 = g - g1.astype(F32)
    g2 = r1.astype(BF)
    g3 = (r1 - g2.astype(F32)).astype(BF)
    return _dot(tri, g1) + _dot(tri, g2) + _dot(tri, g3)


def _gla_chunk(b, q, k, v, gg, state, gn, consts):
    C, tril2, eye, own_lanes = consts
    mid = C // 2
    b_mid = b[mid:mid + 1, :]
    b_last = b[C - 1:C, :]
    qe = (q * jnp.exp(b)).astype(BF)
    qm = (q * jnp.exp(b - b_mid)).astype(BF)
    km = (k * jnp.exp(b_mid - b)).astype(BF)
    kl = k * jnp.exp(b_last - b)
    dec = jnp.exp(b_last)
    out = []
    zeros = jnp.zeros((2 * C, 128), BF)
    for pr in range(GLA_HEADS // 2):
        pls = slice(pr * 128, (pr + 1) * 128)
        v0 = v[:, (2 * pr) * GLA_DV:(2 * pr + 1) * GLA_DV]
        v1 = v[:, (2 * pr + 1) * GLA_DV:(2 * pr + 2) * GLA_DV]
        merged = 2 * C <= GLA_PAIR_MERGE_ROWS
        qm2 = jnp.where(own_lanes, jnp.concatenate([qm[:, pls], qm[:, pls]], axis=0), zeros)
        qe2 = jnp.where(own_lanes, jnp.concatenate([qe[:, pls], qe[:, pls]], axis=0), zeros)
        if merged:
            km2 = jnp.concatenate([km[:, pls], km[:, pls]], axis=0)
            att2 = jnp.where(tril2, _dot_nt(qm2, km2), 0.0).astype(BF)
        s_pair = jnp.concatenate([state[2 * pr], state[2 * pr + 1]], axis=0).astype(BF)
        klt = jnp.transpose(kl[:, pls]).astype(BF)
        dcol = jnp.sum(jnp.where(eye, jnp.broadcast_to(dec[:, pls], (128, 128)), 0.0),
                       axis=1, keepdims=True)
        if merged:
            o2 = _dot(jnp.concatenate([qe2, att2], axis=1), jnp.concatenate([s_pair, v0, v1], axis=0))
            kv2 = _dot(klt, v[:, (2 * pr) * GLA_DV:(2 * pr + 2) * GLA_DV])
        for jj in range(2):
            hh = 2 * pr + jj
            rs = slice(jj * GLA_DK, (jj + 1) * GLA_DK)
            hr = slice(jj * C, (jj + 1) * C)
            vh = (v0, v1)[jj]
            if merged:
                o = o2[hr, :]
                kv = kv2[rs, jj * GLA_DV:(jj + 1) * GLA_DV]
            else:
                att = jnp.where(tril2[0:C, 0:C], _dot_nt(qm2[hr, :], km[:, pls]), 0.0).astype(BF)
                o = _dot(jnp.concatenate([qe2[hr, :], att], axis=1), jnp.concatenate([s_pair, vh], axis=0))
                kv = _dot(klt[rs, :], vh)
            state[hh] = state[hh] * dcol[rs, :] + kv
            out.append((_rms_norm(o, gn) * _silu(gg[:, hh * GLA_DV:(hh + 1) * GLA_DV])).astype(BF))
    return out


def _gla_kernel(q_ref, k_ref, v_ref, la_ref, gg_ref, s0_ref, gn_ref, og_ref, sfin_ref, s_scr, *, C, n_chunks,
                independent):
    NB = q_ref.shape[0]
    if not independent:
        i = pl.program_id(1)

        @pl.when(i == 0)
        def _():
            for nb in range(NB):
                s_scr[nb] = s0_ref[0]

        states = [[s_scr[nb, hh] for hh in range(GLA_HEADS)] for nb in range(NB)]

    consts = _gla_consts(C)
    gn = gn_ref[...]
    b_all = [_gla_cumsum(la_ref[nb], C) for nb in range(NB)]

    for c in range(n_chunks):
        r = slice(c * C, (c + 1) * C)
        for nb in range(NB):
            state = [s0_ref[c, hh] for hh in range(GLA_HEADS)] if independent else states[nb]
            og = _gla_chunk(b_all[nb][r, :], q_ref[nb, r, :], k_ref[nb, r, :], v_ref[nb, r, :], gg_ref[nb, r, :],
                            state, gn, consts)
            for hh in range(GLA_HEADS):
                og_ref[nb, r, hh * GLA_DV:(hh + 1) * GLA_DV] = og[hh]
                if independent:
                    sfin_ref[c, hh] = state[hh]

    if not independent:
        for nb in range(NB):
            for hh in range(GLA_HEADS):
                s_scr[nb, hh] = states[nb][hh]

        @pl.when(i == pl.num_programs(1) - 1)
        def _():
            for nb in range(NB):
                for hh in range(GLA_HEADS):
                    sfin_ref[nb, hh] = states[nb][hh]


def _gla(q, k, v, la, gg, s0, gn, *, B, L, C, tg, rows_per_step=1, independent=False):
    nt = L // tg
    n_chunks = tg // C
    NB = rows_per_step
    blk = lambda b, i: (b, i, 0)
    st = (GLA_HEADS, GLA_DK, GLA_DV)
    if independent:
        assert nt == 1 and NB == 1
        s0_spec = pl.BlockSpec((n_chunks,) + st, lambda b, i: (b, 0, 0, 0))
        sfin_spec = s0_spec
        n_states = B * n_chunks
    else:
        s0_spec = pl.BlockSpec((1,) + st, lambda b, i: (0, 0, 0, 0))
        sfin_spec = pl.BlockSpec((NB,) + st, lambda b, i: (b, 0, 0, 0))
        n_states = B
    in_specs = [pl.BlockSpec((NB, tg, 256), blk), pl.BlockSpec((NB, tg, 256), blk), pl.BlockSpec((NB, tg, 512), blk),
                pl.BlockSpec((NB, tg, 256), blk), pl.BlockSpec((NB, tg, 512), blk),
                s0_spec, pl.BlockSpec((1, GLA_DV), lambda b, i: (0, 0))]
    out_shape = [jax.ShapeDtypeStruct((B, L, 512), BF), jax.ShapeDtypeStruct((n_states,) + st, F32)]
    out_specs = [pl.BlockSpec((NB, tg, 512), blk), sfin_spec]
    r3 = lambda a: a.reshape(B, L, a.shape[-1])
    og, sfin = pl.pallas_call(
        functools.partial(_gla_kernel, C=C, n_chunks=n_chunks, independent=independent),
        out_shape=out_shape, grid=(B // NB, nt), in_specs=in_specs, out_specs=out_specs,
        scratch_shapes=[pltpu.VMEM((NB,) + st, F32)],
        compiler_params=pltpu.CompilerParams(dimension_semantics=("parallel", "arbitrary"),
                                             vmem_limit_bytes=VMEM_LIMIT),
        name="gla_c%d%s" % (C, "_ind" if independent else ""),
    )(r3(q), r3(k), r3(v), r3(la), r3(gg), s0, gn)
    return og.reshape(B * L, 512), sfin


ATTN_TQ = 256
ATTN_SEQ_PER_STEP = 1


def _lanes(x, n):
    if n % 128 == 0:
        return x if n == 128 else jnp.concatenate([x] * (n // 128), axis=1)
    return x[:, 0:n]


def _attn_kernel(qlat_ref, qr_ref, kcat_ref, wuv_ref, *rest, tq, has_meta):
    if has_meta:
        kmeta_ref, o_ref, q_scr, s_scr, p_scr, v_scr, m_scr, l_scr, acc_scr = rest
    else:
        o_ref, q_scr, s_scr, p_scr, v_scr, m_scr, l_scr, acc_scr = rest
    i = pl.program_id(1)
    NB = qlat_ref.shape[0]
    R = MLA_HEADS * tq
    nk = s_scr.shape[2]
    lane_head = lax.broadcasted_iota(jnp.int32, (tq, 256), 1) // MLA_ROPE
    for nb in range(NB):
        qr = qr_ref[nb]
        for hh in range(MLA_HEADS):
            rows = slice(nb * R + hh * tq, nb * R + (hh + 1) * tq)
            q_scr[rows, 0:256] = qlat_ref[nb, :, hh * 256:(hh + 1) * 256]
            q_scr[rows, 256:512] = jnp.where(lane_head == hh, qr, jnp.zeros_like(qr))
    def score(kcs, t, mask, first=False):
        slot = t % 2
        for nb in range(NB):
            kc = kcs(nb)
            rows = slice(nb * R, (nb + 1) * R)
            v_scr[slot, nb] = kc[:, 0:256]
            s = _dot_nt(q_scr[rows, :], kc)
            key = lax.broadcasted_iota(jnp.int32, (R, nk), 1)
            if mask == "causal":
                tok = lax.broadcasted_iota(jnp.int32, (R, nk), 0) % tq
                s = jnp.where(key <= tok, s, NEG_INF)
            elif mask == "meta":
                s = jnp.where(key < N_META, s, NEG_INF)
            s_scr[slot, rows, :] = s
            m_blk = jnp.max(s, axis=1, keepdims=True)
            m_scr[t % 3, rows, :] = (jnp.broadcast_to(m_blk, (R, 128)) if first
                                     else jnp.maximum(m_scr[(t + 2) % 3, rows, :], m_blk))

    def update(t, first=False):
        slot = t % 2
        for nb in range(NB):
            rows = slice(nb * R, (nb + 1) * R)
            m_new = m_scr[t % 3, rows, :]
            p = jnp.exp2(s_scr[slot, rows, :] - _lanes(m_new, nk))
            psum = p if nk < 128 else sum(p[:, c * 128:(c + 1) * 128] for c in range(nk // 128))
            w = psum.shape[1]
            p_scr[rows, :] = p.astype(BF)
            pv = _dot(p_scr[rows, :], v_scr[slot, nb])
            if first:
                if w < 128:
                    l_scr[rows, :] = jnp.zeros((R, 128), F32)
                l_scr[rows, 0:w] = psum
                acc_scr[rows, :] = pv
            else:
                alpha = jnp.exp2(m_scr[(t + 2) % 3, rows, :] - m_new)
                l_scr[rows, 0:w] = alpha[:, 0:w] * l_scr[rows, 0:w] + psum
                acc_scr[rows, :] = _lanes(alpha, 256) * acc_scr[rows, :] + pv

    def keys(blk):
        return lambda nb: kcat_ref[nb, pl.ds(pl.multiple_of(blk * tq, tq), tq), :]

    score(keys(i), 0, "causal", first=True)
    if has_meta:
        score(lambda nb: kmeta_ref[...], 1, "meta")
        update(0, first=True)

        def full(j, carry):
            update(j + 1)
            score(keys(j), j + 2, None)
            return carry

        lax.fori_loop(0, i, full, 0)
        update(i + 1)
    else:
        update(0, first=True)
    for nb in range(NB):
        for pair in range(MLA_HEADS // 2):
            o_pair = []
            for hh in (2 * pair, 2 * pair + 1):
                rows = slice(nb * R + hh * tq, nb * R + (hh + 1) * tq)
                o = acc_scr[rows, :] * (1.0 / jnp.sum(l_scr[rows, :], axis=1, keepdims=True))
                o_pair.append(o.astype(BF))
            o_ref[nb, :, pair * 128:(pair + 1) * 128] = _dot(jnp.concatenate(o_pair, axis=1),
                                                             wuv_ref[pair]).astype(BF)


def _attn(qlat, qr, kcat, kmeta, wuv, *, B, L, tq, seq_per_step=1):
    nq = L // tq
    NB = seq_per_step
    blk = lambda b, i: (b, i, 0)
    in_specs = [pl.BlockSpec((NB, tq, 2048), blk), pl.BlockSpec((NB, tq, 256), blk),
                pl.BlockSpec((NB, L, 512), lambda b, i: (b, 0, 0)),
                pl.BlockSpec((MLA_HEADS // 2, 512, 128), lambda b, i: (0, 0, 0))]
    args = [qlat.reshape(B, L, 2048), qr.reshape(B, L, 256), kcat.reshape(B, L, 512), wuv]
    if kmeta is not None:
        in_specs.append(pl.BlockSpec((tq, 512), lambda b, i: (0, 0)))
        args.append(jnp.pad(kmeta, ((0, tq - N_META), (0, 0))))
    else:
        assert nq == 1
    R = NB * MLA_HEADS * tq
    out = pl.pallas_call(
        functools.partial(_attn_kernel, tq=tq, has_meta=kmeta is not None),
        out_shape=jax.ShapeDtypeStruct((B, L, 512), BF), grid=(B // NB, nq), in_specs=in_specs,
        out_specs=pl.BlockSpec((NB, tq, 512), blk),
        scratch_shapes=[pltpu.VMEM((R, 512), BF), pltpu.VMEM((2, R, tq), F32), pltpu.VMEM((R, tq), BF),
                        pltpu.VMEM((2, NB, tq, 256), BF),
                        pltpu.VMEM((3, R, 128), F32), pltpu.VMEM((R, 128), F32), pltpu.VMEM((R, 256), F32)],
        compiler_params=pltpu.CompilerParams(dimension_semantics=("parallel", "arbitrary"),
                                             vmem_limit_bytes=VMEM_LIMIT),
        name="attn_t%d" % tq,
    )(*args)
    return out.reshape(B * L, 512)


def _uv_kernel(ol_ref, wuv_ref, om_ref):
    for pair in range(MLA_HEADS // 2):
        om_ref[:, pair * 128:(pair + 1) * 128] = _dot(ol_ref[:, pair * 512:(pair + 1) * 512],
                                                      wuv_ref[pair]).astype(BF)


def _uv(ol, wuv):
    T = ol.shape[0]
    return pl.pallas_call(
        _uv_kernel, out_shape=jax.ShapeDtypeStruct((T, 512), BF), grid=(1,),
        in_specs=[pl.BlockSpec((T, 2048), lambda i: (0, 0)),
                  pl.BlockSpec((MLA_HEADS // 2, 512, 128), lambda i: (0, 0, 0))],
        out_specs=pl.BlockSpec((T, 512), lambda i: (0, 0)),
        compiler_params=pltpu.CompilerParams(dimension_semantics=("arbitrary",), vmem_limit_bytes=VMEM_LIMIT),
        name="uv",
    )(ol, wuv)


NEW_PAD = 16
KEY_CH = 1024
KEY_GROUP = 8192
PATTN_SEQ = 2


def _pattn_kernel(pt_ref, ql_ref, qr_ref, cn_ref, kn_ref, cckv_hbm, ckr_hbm, o_ref, ckv_buf, kr_buf, kb_scr,
                  sem, *, n_pages, t_new):
    g = pl.program_id(0)
    ng = pl.num_programs(0)
    slot = g % 2
    n_keys = n_pages * PAGE

    def copies(step, sl):
        out = []
        for j in range(PATTN_SEQ):
            for pg in range(n_pages):
                page = pt_ref[step * PATTN_SEQ + j, pg]
                out.append(pltpu.make_async_copy(
                    cckv_hbm.at[page], ckv_buf.at[sl * PATTN_SEQ + j, pl.ds(pg * PAGE, PAGE), :], sem.at[0, sl]))
                out.append(pltpu.make_async_copy(
                    ckr_hbm.at[page], kr_buf.at[sl * PATTN_SEQ + j, :, pl.ds(pg * PAGE, PAGE)], sem.at[1, sl]))
        return out

    @pl.when(g == 0)
    def _():
        for c in copies(0, 0):
            c.start()

    for c in copies(jnp.minimum(g + 1, ng - 1), 1 - slot):
        c.start()
    for c in copies(g, slot):
        c.wait()

    for j in range(PATTN_SEQ):
        buf = slot * PATTN_SEQ + j
        ql = ql_ref[j]
        qr = qr_ref[j]
        R = ql.shape[0]
        cn = cn_ref[j].astype(BF)
        kn = kn_ref[j].astype(BF)
        s_new = _dot_nt(ql, cn) + _dot_nt(qr, kn)
        tok = lax.broadcasted_iota(jnp.int32, (R, NEW_PAD), 0) % t_new
        key = lax.broadcasted_iota(jnp.int32, (R, NEW_PAD), 1)
        s_new = jnp.where(key <= tok, s_new, NEG_INF)
        m_new = jnp.max(s_new, axis=1, keepdims=True)
        p_new = jnp.exp2(s_new - m_new)
        parts = [(m_new, jnp.sum(p_new, axis=1, keepdims=True), _dot(p_new.astype(BF), cn))]
        for grp in range(n_keys // KEY_GROUP):
            gs = slice(grp * KEY_GROUP, (grp + 1) * KEY_GROUP)
            for ch in range(grp * (KEY_GROUP // KEY_CH), (grp + 1) * (KEY_GROUP // KEY_CH)):
                ks = slice(ch * KEY_CH, (ch + 1) * KEY_CH)
                kb_scr[j, ks, :] = ckv_buf[buf, ks, :].astype(BF)
            s_g = _dot_nt(ql, kb_scr[j, gs, :]) + _dot(qr, kr_buf[buf, :, gs].astype(BF))
            m_g = jnp.max(s_g, axis=1, keepdims=True)
            p_g = jnp.exp2(s_g - m_g)
            parts.append((m_g, jnp.sum(p_g, axis=1, keepdims=True), _dot(p_g.astype(BF), kb_scr[j, gs, :])))
        m = functools.reduce(jnp.maximum, [pt[0] for pt in parts])
        l = sum(pt[1] * jnp.exp2(pt[0] - m) for pt in parts)
        acc = sum(pt[2] * jnp.exp2(pt[0] - m) for pt in parts)
        o_ref[j] = acc * (1.0 / l)

    @pl.when(g == ng - 1)
    def _():
        for c in copies(g, 1 - slot):
            c.wait()


def _pattn(page_table, ql, qr, cn, kn, cache_ckv, cache_krope_t, *, t_new):
    Bd, n_pages = page_table.shape
    R = ql.shape[1]
    n_keys = n_pages * PAGE
    G = PATTN_SEQ
    blk = lambda b, pt: (b, 0, 0)
    grid_spec = pltpu.PrefetchScalarGridSpec(
        num_scalar_prefetch=1, grid=(Bd // G,),
        in_specs=[pl.BlockSpec((G, R, MLA_KV_LORA), blk), pl.BlockSpec((G, R, MLA_ROPE), blk),
                  pl.BlockSpec((G, NEW_PAD, MLA_KV_LORA), blk), pl.BlockSpec((G, NEW_PAD, MLA_ROPE), blk),
                  pl.BlockSpec(memory_space=pl.ANY), pl.BlockSpec(memory_space=pl.ANY)],
        out_specs=pl.BlockSpec((G, R, MLA_KV_LORA), blk),
        scratch_shapes=[pltpu.VMEM((2 * G, n_keys, MLA_KV_LORA), F32), pltpu.VMEM((2 * G, MLA_ROPE, n_keys), F32),
                        pltpu.VMEM((G, n_keys, MLA_KV_LORA), BF), pltpu.SemaphoreType.DMA((2, 2))])
    return pl.pallas_call(
        functools.partial(_pattn_kernel, n_pages=n_pages, t_new=t_new),
        out_shape=jax.ShapeDtypeStruct((Bd, R, MLA_KV_LORA), F32), grid_spec=grid_spec,
        compiler_params=pltpu.CompilerParams(dimension_semantics=("arbitrary",), vmem_limit_bytes=VMEM_LIMIT),
        name="pattn",
    )(page_table, ql, qr, cn, kn, cache_ckv, cache_krope_t)


FFN_TM = 512
FF_CH = 256
N_FF_CH = D_FF // FF_CH


def _ffn_kernel(x_ref, og_ref, om_ref, lng_ref, lnb_ref, wo_ref, l1g_ref, l1b_ref, wf_ref,
                cw_ref, cb_ref, wd_ref, l2g_ref, l2b_ref, *rest, tm, inject):
    if inject:
        inj1_ref, inj2_ref, m1_ref, m2_ref, y_ref, a_ref, act_scr = rest
    else:
        halo0_ref, y_ref, alast_ref, act_scr, halo_scr = rest
        i = pl.program_id(1)

        @pl.when(i == 0)
        def _():
            halo_scr[...] = halo0_ref[...]

    h = _layer_norm(x_ref[...], lng_ref[...], lnb_ref[...])
    mix = _dot(jnp.concatenate([og_ref[...], om_ref[...]], axis=1), wo_ref[...])
    h1 = _layer_norm(DN_ALPHA * h + mix, l1g_ref[...], l1b_ref[...])
    h1b = h1.astype(BF)
    ridx = lax.broadcasted_iota(jnp.int32, (tm, FF_CH), 0)
    for c in range(N_FF_CH):
        cs = slice(c * FF_CH, (c + 1) * FF_CH)
        a = _dot(h1b, wf_ref[:, cs])
        up = _dot(h1b, wf_ref[:, D_FF + c * FF_CH:D_FF + (c + 1) * FF_CH])
        r1 = pltpu.roll(a, 1, 0)
        r2 = pltpu.roll(a, 2, 0)
        if inject:
            a1 = jnp.where(m1_ref[...] > 0.0, inj1_ref[:, cs], r1)
            a2 = jnp.where(m2_ref[...] > 0.0, inj2_ref[:, cs], r2)
            a_ref[:, cs] = a
        else:
            hl = halo_scr[:, cs]
            a1 = jnp.where(ridx == 0, hl[1:2, :], r1)
            a2 = jnp.where(ridx == 0, hl[0:1, :], jnp.where(ridx == 1, hl[1:2, :], r2))
            halo_scr[:, cs] = a[tm - 2:tm, :]
            alast_ref[0, :, cs] = a[tm - 2:tm, :]
        conv = cb_ref[:, cs] + cw_ref[0:1, cs] * a2 + cw_ref[1:2, cs] * a1 + cw_ref[2:3, cs] * a
        act_scr[:, cs] = (_silu(conv) * up).astype(BF)
    ffn = _dot(act_scr[...], wd_ref[...])
    y_ref[...] = _layer_norm(DN_ALPHA * h1 + ffn, l2g_ref[...], l2b_ref[...])


def _ffn(x, og, om, w, *, B, L, tm, halo0=None, inject=None):
    nt = L // tm
    T = B * L
    row = lambda b, i: (b * nt + i, 0)
    c2 = lambda b, i: (0, 0)
    once = pl.Buffered(1)
    in_specs = [pl.BlockSpec((tm, D_MODEL), row), pl.BlockSpec((tm, 512), row), pl.BlockSpec((tm, 512), row),
                pl.BlockSpec((1, D_MODEL), c2), pl.BlockSpec((1, D_MODEL), c2),
                pl.BlockSpec((D_MODEL, D_MODEL), c2, pipeline_mode=once),
                pl.BlockSpec((1, D_MODEL), c2), pl.BlockSpec((1, D_MODEL), c2),
                pl.BlockSpec((D_MODEL, 2 * D_FF), c2, pipeline_mode=once),
                pl.BlockSpec((3, D_FF), c2), pl.BlockSpec((1, D_FF), c2),
                pl.BlockSpec((D_FF, D_MODEL), c2, pipeline_mode=once),
                pl.BlockSpec((1, D_MODEL), c2), pl.BlockSpec((1, D_MODEL), c2)]
    args = [x, og, om, w["lng"], w["lnb"], w["wo"], w["l1g"], w["l1b"], w["wf"],
            w["cw"], w["cb"], w["wd"], w["l2g"], w["l2b"]]
    scratch = [pltpu.VMEM((tm, D_FF), BF)]
    if inject is not None:
        inj1, inj2, m1, m2 = inject
        in_specs += [pl.BlockSpec((tm, D_FF), row), pl.BlockSpec((tm, D_FF), row),
                     pl.BlockSpec((tm, 1), row), pl.BlockSpec((tm, 1), row)]
        args += [inj1, inj2, m1, m2]
        out_shape = [jax.ShapeDtypeStruct((T, D_MODEL), F32), jax.ShapeDtypeStruct((T, D_FF), F32)]
        out_specs = [pl.BlockSpec((tm, D_MODEL), row), pl.BlockSpec((tm, D_FF), row)]
    else:
        in_specs.append(pl.BlockSpec((2, D_FF), c2))
        args.append(halo0)
        out_shape = [jax.ShapeDtypeStruct((T, D_MODEL), F32), jax.ShapeDtypeStruct((B, 2, D_FF), F32)]
        out_specs = [pl.BlockSpec((tm, D_MODEL), row), pl.BlockSpec((1, 2, D_FF), lambda b, i: (b, 0, 0))]
        scratch.append(pltpu.VMEM((2, D_FF), F32))
    return pl.pallas_call(
        functools.partial(_ffn_kernel, tm=tm, inject=inject is not None),
        out_shape=out_shape, grid=(B, nt), in_specs=in_specs, out_specs=out_specs, scratch_shapes=scratch,
        compiler_params=pltpu.CompilerParams(dimension_semantics=("parallel", "arbitrary"),
                                             vmem_limit_bytes=VMEM_LIMIT),
        name="ffn_inject" if inject is not None else "ffn_seq",
    )(*args)


def _rope_tables(pos):
    half = MLA_ROPE // 2
    inv = ROPE_BASE ** (-jnp.arange(half, dtype=F32) / half)
    ang = pos.astype(F32)[:, None] * inv[None, :]
    cos = jnp.concatenate([jnp.cos(ang)] * 2, axis=1)
    sin = jnp.concatenate([jnp.sin(ang)] * 2, axis=1)
    pad = jnp.zeros((pos.shape[0], 128 - MLA_ROPE), F32)
    return (jnp.tile(cos, (1, MLA_HEADS)), jnp.tile(sin, (1, MLA_HEADS)),
            jnp.concatenate([cos, pad], axis=1), jnp.concatenate([sin, pad], axis=1))


def _swap_halves(w):
    half = w.shape[-1] // 2
    return jnp.concatenate([-w[..., half:], w[..., :half]], axis=-1)


def _prep_weights(ln_emb_g, ln_emb_b, w_in, w_gate_up, b_gate, gla_norm_g, mla_q_norm_g, mla_kv_norm_g,
                  w_uq, w_uk, w_uv, w_o, ln1_g, ln1_b, w_ffn_in, conv_w, conv_b, w_down, ln2_g, ln2_b):
    o_alr, o_cq, o_ckv, o_kr = 1536, 1552, 1936, 2192
    w_kr = w_in[:, o_kr:o_kr + MLA_ROPE]
    last = jnp.concatenate([w_kr, _swap_halves(w_kr), w_in[:, o_alr:o_cq],
                            jnp.zeros((D_MODEL, 128 - 2 * MLA_ROPE - GLA_RANK), F32)], axis=1)
    win = jnp.concatenate([w_in[:, o_cq:o_ckv], last, w_in[:, o_ckv:o_kr], w_in[:, :o_alr]],
                          axis=1).astype(BF)
    wg = jnp.zeros((128, 256), F32).at[LAST_ALR:LAST_ALR + GLA_RANK].set(w_gate_up).astype(BF)
    uq = w_uq.reshape(MLA_Q_LORA, MLA_HEADS, MLA_NOPE + MLA_ROPE)
    uq_r = uq[:, :, MLA_NOPE:]
    wuq = jnp.concatenate([uq[:, :, :MLA_NOPE].reshape(MLA_Q_LORA, 512), uq_r.reshape(MLA_Q_LORA, 256),
                           _swap_halves(uq_r).reshape(MLA_Q_LORA, 256)], axis=1).astype(BF)
    ukt = jnp.transpose(w_uk.reshape(MLA_KV_LORA, MLA_HEADS // 2, 2, MLA_NOPE), (1, 2, 3, 0))
    z64 = jnp.zeros_like(ukt[:, 0])
    wuk = jnp.concatenate([jnp.concatenate([ukt[:, 0], z64], axis=2),
                           jnp.concatenate([z64, ukt[:, 1]], axis=2)], axis=1).astype(BF)
    uv = jnp.transpose(w_uv.reshape(MLA_KV_LORA, MLA_HEADS, MLA_DV), (1, 0, 2))
    uv = uv.reshape(MLA_HEADS // 2, 2, MLA_KV_LORA, MLA_DV)
    zz = jnp.zeros_like(uv[:, 0])
    wuv = jnp.concatenate([jnp.concatenate([uv[:, 0], zz], axis=2), jnp.concatenate([zz, uv[:, 1]], axis=2)],
                          axis=1).astype(BF)
    r = lambda a: a.reshape(1, -1)
    return dict(lng=r(ln_emb_g), lnb=r(ln_emb_b), win=win, wg=wg, bg=r(b_gate), qng=r(mla_q_norm_g),
                kvg=r(mla_kv_norm_g), wuq=wuq, wuk=wuk, wuv=wuv, wo=w_o.astype(BF), l1g=r(ln1_g), l1b=r(ln1_b),
                wf=w_ffn_in.astype(BF), cw=conv_w, cb=r(conv_b),
                wd=w_down.astype(BF), l2g=r(ln2_g), l2b=r(ln2_b), gn=r(gla_norm_g))


def kernel(x_prompt, x_sample, cache_ckv, cache_krope, page_table, state_gla, state_conv, meta_tokens, ln_emb_g, ln_emb_b, w_in, w_gate_up, b_gate, gla_norm_g, mla_q_norm_g, mla_kv_norm_g, w_uq, w_uk, w_uv, w_o, ln1_g, ln1_b, w_ffn_in, conv_w, conv_b, w_down, ln2_g, ln2_b):
    B, S, _ = x_prompt.shape
    Bd, Td, _ = x_sample.shape
    past_len = page_table.shape[1] * PAGE
    n_small = N_META + Bd * Td
    w = _prep_weights(ln_emb_g, ln_emb_b, w_in, w_gate_up, b_gate, gla_norm_g, mla_q_norm_g, mla_kv_norm_g,
                      w_uq, w_uk, w_uv, w_o, ln1_g, ln1_b, w_ffn_in, conv_w, conv_b, w_down, ln2_g, ln2_b)

    xs = jnp.concatenate([meta_tokens.astype(F32), x_sample.reshape(Bd * Td, D_MODEL)], axis=0)
    pos_s = jnp.concatenate([jnp.arange(N_META, dtype=jnp.int32),
                             jnp.tile(past_len + jnp.arange(Td, dtype=jnp.int32), Bd)])
    (qg_s, kg_s, vg_s, la_s, gg_s, qlat_s, qr_s, ckv_s, kr_s, kcat_s) = _proj(
        xs, w, _rope_tables(pos_s), B=1, L=n_small, tm=n_small)

    M = N_META
    zero_state = jnp.zeros((1, GLA_HEADS, GLA_DK, GLA_DV), F32)
    og_m, s_meta = _gla(qg_s[:M], kg_s[:M], vg_s[:M], la_s[:M], gg_s[:M], zero_state, w["gn"],
                        B=1, L=M, C=M, tg=M)
    om_m = _attn(qlat_s[:M], qr_s[:M], kcat_s[:M], None, w["wuv"], B=1, L=M, tq=M)

    TP = 16
    padt = lambda a: jnp.pad(a[M:].reshape(Bd, Td, -1), ((0, 0), (0, TP - Td), (0, 0))).reshape(Bd * TP, -1)
    og_sp, gla_sample = _gla(padt(qg_s), padt(kg_s), padt(vg_s), padt(la_s), padt(gg_s), state_gla, w["gn"],
                             B=Bd // GLA_SEQ_PER_STEP, L=GLA_SEQ_PER_STEP * TP, C=TP, tg=GLA_SEQ_PER_STEP * TP,
                             independent=True)
    og_s = og_sp.reshape(Bd, TP, 512)[:, :Td].reshape(Bd * Td, 512)

    heads_first = lambda a, d: jnp.transpose(a[M:].reshape(Bd, Td, MLA_HEADS, d), (0, 2, 1, 3)).reshape(
        Bd, MLA_HEADS * Td, d)
    padn = lambda a: jnp.pad(a[M:].reshape(Bd, Td, -1), ((0, 0), (0, NEW_PAD - Td), (0, 0)))
    o_s = _pattn(page_table, heads_first(qlat_s, MLA_KV_LORA), heads_first(qr_s, MLA_ROPE),
                 padn(ckv_s), padn(kr_s), cache_ckv, jnp.swapaxes(cache_krope, 1, 2), t_new=Td)
    om_s = _uv(jnp.transpose(o_s.reshape(Bd, MLA_HEADS, Td, MLA_KV_LORA), (0, 2, 1, 3)).reshape(
        Bd * Td, MLA_HEADS * MLA_KV_LORA).astype(BF), w["wuv"])

    sc = state_conv.astype(F32)
    zrow = jnp.zeros((Bd, 1, D_FF), F32)
    inj1 = jnp.concatenate([sc[:, 1:2], zrow, zrow, zrow], axis=1).reshape(Bd * Td, D_FF)
    inj2 = jnp.concatenate([sc[:, 0:1], sc[:, 1:2], zrow, zrow], axis=1).reshape(Bd * Td, D_FF)
    zmeta = jnp.zeros((M, D_FF), F32)
    t_in = np.concatenate([np.arange(M), np.tile(np.arange(Td), Bd)])
    m1 = jnp.asarray((t_in == 0).astype(np.float32)[:, None])
    m2 = jnp.asarray((t_in <= 1).astype(np.float32)[:, None])
    y_s, a_s = _ffn(xs, jnp.concatenate([og_m, og_s], axis=0), jnp.concatenate([om_m, om_s], axis=0), w,
                    B=1, L=n_small, tm=176,
                    inject=(jnp.concatenate([zmeta, inj1], axis=0), jnp.concatenate([zmeta, inj2], axis=0), m1, m2))

    TM = 512
    xp = x_prompt.reshape(B * S, D_MODEL)
    pos_p = N_META + jnp.arange(S, dtype=jnp.int32)
    (qg_p, kg_p, vg_p, la_p, gg_p, qlat_p, qr_p, ckv_p, kr_p, kcat_p) = _proj(
        xp, w, _rope_tables(pos_p), B=B, L=S, tm=TM)
    og_p, gla_p = _gla(qg_p, kg_p, vg_p, la_p, gg_p, s_meta, w["gn"], B=B, L=S, C=GLA_CHUNK, tg=TM,
                       rows_per_step=GLA_ROWS_PER_STEP)
    om_p = _attn(qlat_p, qr_p, kcat_p, kcat_s[:M], w["wuv"], B=B, L=S, tq=ATTN_TQ, seq_per_step=ATTN_SEQ_PER_STEP)
    y_p, conv_prompt = _ffn(xp, og_p, om_p, w, B=B, L=S, tm=FFN_TM, halo0=a_s[M - 2:M])

    y_prompt = y_p.reshape(B, S, D_MODEL)
    y_sample = y_s[M:].reshape(Bd, Td, D_MODEL)
    ckv_prompt = jnp.concatenate([jnp.broadcast_to(ckv_s[None, :M], (B, M, MLA_KV_LORA)),
                                  ckv_p.reshape(B, S, MLA_KV_LORA)], axis=1)
    krope_prompt = jnp.concatenate([jnp.broadcast_to(kr_s[None, :M], (B, M, MLA_ROPE)),
                                    kr_p.reshape(B, S, MLA_ROPE)], axis=1)
    ckv_sample = ckv_s[M:].reshape(Bd, Td, MLA_KV_LORA)
    krope_sample = kr_s[M:].reshape(Bd, Td, MLA_ROPE)
    conv_sample = a_s[M:].reshape(Bd, Td, D_FF)[:, Td - 2:]
    return (y_prompt, y_sample, ckv_prompt, krope_prompt, gla_p, conv_prompt,
            ckv_sample, krope_sample, gla_sample, conv_sample)
```

```python
import functools

import numpy as np
import jax
import jax.numpy as jnp
from jax import lax
from jax.experimental import pallas as pl
from jax.experimental.pallas import tpu as pltpu

F32 = jnp.float32
BF = jnp.bfloat16

D_MODEL = 1024
N_META = 16
GLA_HEADS = 4
GLA_DK = 64
GLA_DV = 128
GLA_RANK = 16
GLA_TAU = 16.0
GLA_CHUNK = 64
MLA_HEADS = 8
MLA_DV = 64
MLA_NOPE = 64
MLA_ROPE = 32
MLA_Q_LORA = 384
MLA_KV_LORA = 256
MLA_SCALE = (MLA_NOPE + MLA_ROPE) ** -0.5
Q_SCALE = MLA_SCALE * 1.4426950408889634
ROPE_BASE = 10000.0
D_FF = 2816
DN_ALPHA = 2.0 ** 0.25
NORM_EPS = 1e-5
NEG_INF = -1e30
PAGE = 128

C_CQ, C_LAST, C_CKV, C_Q, C_K, C_V, C_G, C_END = 0, 384, 512, 768, 1024, 1280, 1792, 2304
LAST_ALR = 64

VMEM_LIMIT = 56 * 1024 * 1024


def _dot(a, b):
    return jnp.dot(a, b, preferred_element_type=F32)


def _dot_nt(a, b):
    return lax.dot_general(a, b, (((1,), (1,)), ((), ())), preferred_element_type=F32)


def _layer_norm(x, g, b):
    mu = jnp.mean(x, axis=-1, keepdims=True)
    xc = x - mu
    var = jnp.mean(xc * xc, axis=-1, keepdims=True)
    return xc * lax.rsqrt(var + NORM_EPS) * g + b


def _rms_norm(x, g):
    ms = jnp.mean(x * x, axis=-1, keepdims=True)
    return x * lax.rsqrt(ms + NORM_EPS) * g


def _silu(x):
    return x * (1.0 / (1.0 + jnp.exp(-x)))


def _proj_kernel(x_ref, lng_ref, lnb_ref, win_ref, wg_ref, bg_ref, qng_ref, kvg_ref, wuq_ref, wuk_ref,
                 cq_ref, sq_ref, ck_ref, sk_ref,
                 qg_ref, kg_ref, vg_ref, la_ref, gg_ref, qlat_ref, qr_ref, ckv_ref, kr_ref, kcat_ref):
    h = _layer_norm(x_ref[...], lng_ref[...], lnb_ref[...])
    z = _dot(h.astype(BF), win_ref[...])
    qg_ref[...] = z[:, C_Q:C_K] * (GLA_DK ** -0.5)
    kg_ref[...] = z[:, C_K:C_V]
    vg_ref[...] = z[:, C_V:C_G].astype(BF)
    gg_ref[...] = z[:, C_G:C_END]
    last = z[:, C_LAST:C_LAST + 128]
    y = _dot(last.astype(BF), wg_ref[...]) + bg_ref[...]
    la_ref[...] = (jnp.minimum(y, 0.0) - jnp.log1p(jnp.exp(-jnp.abs(y)))) * (1.0 / GLA_TAU)
    cqn = _rms_norm(z[:, C_CQ:C_CQ + MLA_Q_LORA], qng_ref[...])
    qa = _dot(cqn.astype(BF), wuq_ref[...])
    qn = (qa[:, 0:512] * Q_SCALE).astype(BF)
    for pair in range(MLA_HEADS // 2):
        qlat_ref[:, pair * 512:(pair + 1) * 512] = _dot(qn[:, pair * 128:(pair + 1) * 128], wuk_ref[pair]).astype(BF)
    qr = qa[:, 512:768] * cq_ref[...] + qa[:, 768:1024] * sq_ref[...]
    qr_ref[...] = (qr * Q_SCALE).astype(BF)
    ckv = _rms_norm(z[:, C_CKV:C_CKV + MLA_KV_LORA], kvg_ref[...])
    ckv_ref[...] = ckv
    kr = last * ck_ref[...] + pltpu.roll(last, 96, 1) * sk_ref[...]
    kr_ref[...] = kr[:, 0:MLA_ROPE]
    kr2 = kr + pltpu.roll(kr, 32, 1)
    kr4 = kr2 + pltpu.roll(kr2, 64, 1)
    kcat_ref[:, 0:256] = ckv.astype(BF)
    kcat_ref[:, 256:384] = kr4.astype(BF)
    kcat_ref[:, 384:512] = kr4.astype(BF)


def _proj(x, w, tabs, *, B, L, tm):
    T = B * L
    nt = L // tm
    cq, sq, ck, sk = tabs
    row = lambda b, i: (b * nt + i, 0)
    tab = lambda b, i: (i, 0)
    c2 = lambda b, i: (0, 0)
    c3 = lambda b, i: (0, 0, 0)
    in_specs = [
        pl.BlockSpec((tm, D_MODEL), row),
        pl.BlockSpec((1, D_MODEL), c2), pl.BlockSpec((1, D_MODEL), c2),
        pl.BlockSpec((D_MODEL, C_END), c2),
        pl.BlockSpec((128, 256), c2), pl.BlockSpec((1, 256), c2),
        pl.BlockSpec((1, MLA_Q_LORA), c2), pl.BlockSpec((1, MLA_KV_LORA), c2),
        pl.BlockSpec((MLA_Q_LORA, 1024), c2),
        pl.BlockSpec((MLA_HEADS // 2, 128, 512), c3),
        pl.BlockSpec((tm, 256), tab), pl.BlockSpec((tm, 256), tab),
        pl.BlockSpec((tm, 128), tab), pl.BlockSpec((tm, 128), tab),
    ]
    args = [x, w["lng"], w["lnb"], w["win"], w["wg"], w["bg"], w["qng"], w["kvg"], w["wuq"], w["wuk"], cq, sq, ck, sk]
    outs = [(256, F32), (256, F32), (512, BF), (256, F32), (512, F32), (2048, BF), (256, BF), (256, F32),
            (MLA_ROPE, F32), (512, BF)]
    out_shape = [jax.ShapeDtypeStruct((T, wd), dt) for wd, dt in outs]
    out_specs = [pl.BlockSpec((tm, wd), row) for wd, _ in outs]
    return pl.pallas_call(
        _proj_kernel, out_shape=out_shape, grid=(B, nt), in_specs=in_specs, out_specs=out_specs,
        compiler_params=pltpu.CompilerParams(dimension_semantics=("parallel", "parallel"),
                                             vmem_limit_bytes=VMEM_LIMIT),
        name="proj",
    )(*args)


GLA_SEQ_PER_STEP = 8
GLA_ROWS_PER_STEP = 2
GLA_PAIR_MERGE_ROWS = 32


def _gla_consts(C):
    row = lax.broadcasted_iota(jnp.int32, (2 * C, 2 * C), 0)
    col = lax.broadcasted_iota(jnp.int32, (2 * C, 2 * C), 1)
    tril2 = (col <= row) & (col >= (row // C) * C)
    eye = (lax.broadcasted_iota(jnp.int32, (128, 128), 0) == lax.broadcasted_iota(jnp.int32, (128, 128), 1))
    first_half = ((lax.broadcasted_iota(jnp.int32, (2 * C, 128), 1) < GLA_DK)
                  == (lax.broadcasted_iota(jnp.int32, (2 * C, 128), 0) < C))
    return C, tril2, eye, first_half


def _gla_cumsum(g, C):
    n = g.shape[0]
    row = lax.broadcasted_iota(jnp.int32, (n, n), 0)
    col = lax.broadcasted_iota(jnp.int32, (n, n), 1)
    tri = jnp.where((col <= row) & (col >= (row // C) * C), 1.0, 0.0).astype(BF)
    g1 = g.astype(BF)
    r1 = g - g1.astype(F32)
    g2 = r1.astype(BF)
    g3 = (r1 - g2.astype(F32)).astype(BF)
    return _dot(tri, g1) + _dot(tri, g2) + _dot(tri, g3)


def _gla_chunk(b, q, k, v, gg, state, gn, consts):
    C, tril2, eye, own_lanes = consts
    mid = C // 2
    b_mid = b[mid:mid + 1, :]
    b_last = b[C - 1:C, :]
    qe = (q * jnp.exp(b)).astype(BF)
    qm = (q * jnp.exp(b - b_mid)).astype(BF)
    km = (k * jnp.exp(b_mid - b)).astype(BF)
    kl = k * jnp.exp(b_last - b)
    dec = jnp.exp(b_last)
    out = []
    zeros = jnp.zeros((2 * C, 128), BF)
    for pr in range(GLA_HEADS // 2):
        pls = slice(pr * 128, (pr + 1) * 128)
        v0 = v[:, (2 * pr) * GLA_DV:(2 * pr + 1) * GLA_DV]
        v1 = v[:, (2 * pr + 1) * GLA_DV:(2 * pr + 2) * GLA_DV]
        merged = 2 * C <= GLA_PAIR_MERGE_ROWS
        qm2 = jnp.where(own_lanes, jnp.concatenate([qm[:, pls], qm[:, pls]], axis=0), zeros)
        qe2 = jnp.where(own_lanes, jnp.concatenate([qe[:, pls], qe[:, pls]], axis=0), zeros)
        if merged:
            km2 = jnp.concatenate([km[:, pls], km[:, pls]], axis=0)
            att2 = jnp.where(tril2, _dot_nt(qm2, km2), 0.0).astype(BF)
        s_pair = jnp.concatenate([state[2 * pr], state[2 * pr + 1]], axis=0).astype(BF)
        klt = jnp.transpose(kl[:, pls]).astype(BF)
        dcol = jnp.sum(jnp.where(eye, jnp.broadcast_to(dec[:, pls], (128, 128)), 0.0),
                       axis=1, keepdims=True)
        if merged:
            o2 = _dot(jnp.concatenate([qe2, att2], axis=1), jnp.concatenate([s_pair, v0, v1], axis=0))
            kv2 = _dot(klt, v[:, (2 * pr) * GLA_DV:(2 * pr + 2) * GLA_DV])
        for jj in range(2):
            hh = 2 * pr + jj
            rs = slice(jj * GLA_DK, (jj + 1) * GLA_DK)
            hr = slice(jj * C, (jj + 1) * C)
            vh = (v0, v1)[jj]
            if merged:
                o = o2[hr, :]
                kv = kv2[rs, jj * GLA_DV:(jj + 1) * GLA_DV]
            else:
                att = jnp.where(tril2[0:C, 0:C], _dot_nt(qm2[hr, :], km[:, pls]), 0.0).astype(BF)
                o = _dot(jnp.concatenate([qe2[hr, :], att], axis=1), jnp.concatenate([s_pair, vh], axis=0))
                kv = _dot(klt[rs, :], vh)
            state[hh] = state[hh] * dcol[rs, :] + kv
            out.append((_rms_norm(o, gn) * _silu(gg[:, hh * GLA_DV:(hh + 1) * GLA_DV])).astype(BF))
    return out


def _gla_kernel(q_ref, k_ref, v_ref, la_ref, gg_ref, s0_ref, gn_ref, og_ref, sfin_ref, s_scr, *, C, n_chunks,
                independent):
    NB = q_ref.shape[0]
    if not independent:
        i = pl.program_id(1)

        @pl.when(i == 0)
        def _():
            for nb in range(NB):
                s_scr[nb] = s0_ref[0]

        states = [[s_scr[nb, hh] for hh in range(GLA_HEADS)] for nb in range(NB)]

    consts = _gla_consts(C)
    gn = gn_ref[...]
    b_all = [_gla_cumsum(la_ref[nb], C) for nb in range(NB)]

    for c in range(n_chunks):
        r = slice(c * C, (c + 1) * C)
        for nb in range(NB):
            state = [s0_ref[c, hh] for hh in range(GLA_HEADS)] if independent else states[nb]
            og = _gla_chunk(b_all[nb][r, :], q_ref[nb, r, :], k_ref[nb, r, :], v_ref[nb, r, :], gg_ref[nb, r, :],
                            state, gn, consts)
            for hh in range(GLA_HEADS):
                og_ref[nb, r, hh * GLA_DV:(hh + 1) * GLA_DV] = og[hh]
                if independent:
                    sfin_ref[c, hh] = state[hh]

    if not independent:
        for nb in range(NB):
            for hh in range(GLA_HEADS):
                s_scr[nb, hh] = states[nb][hh]

        @pl.when(i == pl.num_programs(1) - 1)
        def _():
            for nb in range(NB):
                for hh in range(GLA_HEADS):
                    sfin_ref[nb, hh] = states[nb][hh]


def _gla(q, k, v, la, gg, s0, gn, *, B, L, C, tg, rows_per_step=1, independent=False):
    nt = L // tg
    n_chunks = tg // C
    NB = rows_per_step
    blk = lambda b, i: (b, i, 0)
    st = (GLA_HEADS, GLA_DK, GLA_DV)
    if independent:
        assert nt == 1 and NB == 1
        s0_spec = pl.BlockSpec((n_chunks,) + st, lambda b, i: (b, 0, 0, 0))
        sfin_spec = s0_spec
        n_states = B * n_chunks
    else:
        s0_spec = pl.BlockSpec((1,) + st, lambda b, i: (0, 0, 0, 0))
        sfin_spec = pl.BlockSpec((NB,) + st, lambda b, i: (b, 0, 0, 0))
        n_states = B
    in_specs = [pl.BlockSpec((NB, tg, 256), blk), pl.BlockSpec((NB, tg, 256), blk), pl.BlockSpec((NB, tg, 512), blk),
                pl.BlockSpec((NB, tg, 256), blk), pl.BlockSpec((NB, tg, 512), blk),
                s0_spec, pl.BlockSpec((1, GLA_DV), lambda b, i: (0, 0))]
    out_shape = [jax.ShapeDtypeStruct((B, L, 512), BF), jax.ShapeDtypeStruct((n_states,) + st, F32)]
    out_specs = [pl.BlockSpec((NB, tg, 512), blk), sfin_spec]
    r3 = lambda a: a.reshape(B, L, a.shape[-1])
    og, sfin = pl.pallas_call(
        functools.partial(_gla_kernel, C=C, n_chunks=n_chunks, independent=independent),
        out_shape=out_shape, grid=(B // NB, nt), in_specs=in_specs, out_specs=out_specs,
        scratch_shapes=[pltpu.VMEM((NB,) + st, F32)],
        compiler_params=pltpu.CompilerParams(dimension_semantics=("parallel", "arbitrary"),
                                             vmem_limit_bytes=VMEM_LIMIT),
        name="gla_c%d%s" % (C, "_ind" if independent else ""),
    )(r3(q), r3(k), r3(v), r3(la), r3(gg), s0, gn)
    return og.reshape(B * L, 512), sfin


ATTN_TQ = 256
ATTN_SEQ_PER_STEP = 1


def _lanes(x, n):
    if n % 128 == 0:
        return x if n == 128 else jnp.concatenate([x] * (n // 128), axis=1)
    return x[:, 0:n]


def _attn_kernel(qlat_ref, qr_ref, kcat_ref, wuv_ref, *rest, tq, has_meta):
    if has_meta:
        kmeta_ref, o_ref, q_scr, s_scr, p_scr, v_scr, m_scr, l_scr, acc_scr = rest
    else:
        o_ref, q_scr, s_scr, p_scr, v_scr, m_scr, l_scr, acc_scr = rest
    i = pl.program_id(1)
    NB = qlat_ref.shape[0]
    R = MLA_HEADS * tq
    nk = s_scr.shape[2]
    lane_head = lax.broadcasted_iota(jnp.int32, (tq, 256), 1) // MLA_ROPE
    for nb in range(NB):
        qr = qr_ref[nb]
        for hh in range(MLA_HEADS):
            rows = slice(nb * R + hh * tq, nb * R + (hh + 1) * tq)
            q_scr[rows, 0:256] = qlat_ref[nb, :, hh * 256:(hh + 1) * 256]
            q_scr[rows, 256:512] = jnp.where(lane_head == hh, qr, jnp.zeros_like(qr))
    def score(kcs, t, mask, first=False):
        slot = t % 2
        for nb in range(NB):
            kc = kcs(nb)
            rows = slice(nb * R, (nb + 1) * R)
            v_scr[slot, nb] = kc[:, 0:256]
            s = _dot_nt(q_scr[rows, :], kc)
            key = lax.broadcasted_iota(jnp.int32, (R, nk), 1)
            if mask == "causal":
                tok = lax.broadcasted_iota(jnp.int32, (R, nk), 0) % tq
                s = jnp.where(key <= tok, s, NEG_INF)
            elif mask == "meta":
                s = jnp.where(key < N_META, s, NEG_INF)
            s_scr[slot, rows, :] = s
            m_blk = jnp.max(s, axis=1, keepdims=True)
            m_scr[t % 3, rows, :] = (jnp.broadcast_to(m_blk, (R, 128)) if first
                                     else jnp.maximum(m_scr[(t + 2) % 3, rows, :], m_blk))

    def update(t, first=False):
        slot = t % 2
        for nb in range(NB):
            rows = slice(nb * R, (nb + 1) * R)
            m_new = m_scr[t % 3, rows, :]
            p = jnp.exp2(s_scr[slot, rows, :] - _lanes(m_new, nk))
            psum = p if nk < 128 else sum(p[:, c * 128:(c + 1) * 128] for c in range(nk // 128))
            w = psum.shape[1]
            p_scr[rows, :] = p.astype(BF)
            pv = _dot(p_scr[rows, :], v_scr[slot, nb])
            if first:
                if w < 128:
                    l_scr[rows, :] = jnp.zeros((R, 128), F32)
                l_scr[rows, 0:w] = psum
                acc_scr[rows, :] = pv
            else:
                alpha = jnp.exp2(m_scr[(t + 2) % 3, rows, :] - m_new)
                l_scr[rows, 0:w] = alpha[:, 0:w] * l_scr[rows, 0:w] + psum
                acc_scr[rows, :] = _lanes(alpha, 256) * acc_scr[rows, :] + pv

    def keys(blk):
        return lambda nb: kcat_ref[nb, pl.ds(pl.multiple_of(blk * tq, tq), tq), :]

    score(keys(i), 0, "causal", first=True)
    if has_meta:
        score(lambda nb: kmeta_ref[...], 1, "meta")
        update(0, first=True)

        def full(j, carry):
            update(j + 1)
            score(keys(j), j + 2, None)
            return carry

        lax.fori_loop(0, i, full, 0)
        update(i + 1)
    else:
        update(0, first=True)
    for nb in range(NB):
        for pair in range(MLA_HEADS // 2):
            o_pair = []
            for hh in (2 * pair, 2 * pair + 1):
                rows = slice(nb * R + hh * tq, nb * R + (hh + 1) * tq)
                o = acc_scr[rows, :] * (1.0 / jnp.sum(l_scr[rows, :], axis=1, keepdims=True))
                o_pair.append(o.astype(BF))
            o_ref[nb, :, pair * 128:(pair + 1) * 128] = _dot(jnp.concatenate(o_pair, axis=1),
                                                             wuv_ref[pair]).astype(BF)


def _attn(qlat, qr, kcat, kmeta, wuv, *, B, L, tq, seq_per_step=1):
    nq = L // tq
    NB = seq_per_step
    blk = lambda b, i: (b, i, 0)
    in_specs = [pl.BlockSpec((NB, tq, 2048), blk), pl.BlockSpec((NB, tq, 256), blk),
                pl.BlockSpec((NB, L, 512), lambda b, i: (b, 0, 0)),
                pl.BlockSpec((MLA_HEADS // 2, 512, 128), lambda b, i: (0, 0, 0))]
    args = [qlat.reshape(B, L, 2048), qr.reshape(B, L, 256), kcat.reshape(B, L, 512), wuv]
    if kmeta is not None:
        in_specs.append(pl.BlockSpec((tq, 512), lambda b, i: (0, 0)))
        args.append(jnp.pad(kmeta, ((0, tq - N_META), (0, 0))))
    else:
        assert nq == 1
    R = NB * MLA_HEADS * tq
    out = pl.pallas_call(
        functools.partial(_attn_kernel, tq=tq, has_meta=kmeta is not None),
        out_shape=jax.ShapeDtypeStruct((B, L, 512), BF), grid=(B // NB, nq), in_specs=in_specs,
        out_specs=pl.BlockSpec((NB, tq, 512), blk),
        scratch_shapes=[pltpu.VMEM((R, 512), BF), pltpu.VMEM((2, R, tq), F32), pltpu.VMEM((R, tq), BF),
                        pltpu.VMEM((2, NB, tq, 256), BF),
                        pltpu.VMEM((3, R, 128), F32), pltpu.VMEM((R, 128), F32), pltpu.VMEM((R, 256), F32)],
        compiler_params=pltpu.CompilerParams(dimension_semantics=("parallel", "arbitrary"),
                                             vmem_limit_bytes=VMEM_LIMIT),
        name="attn_t%d" % tq,
    )(*args)
    return out.reshape(B * L, 512)


def _uv_kernel(ol_ref, wuv_ref, om_ref):
    for pair in range(MLA_HEADS // 2):
        om_ref[:, pair * 128:(pair + 1) * 128] = _dot(ol_ref[:, pair * 512:(pair + 1) * 512],
                                                      wuv_ref[pair]).astype(BF)


def _uv(ol, wuv):
    T = ol.shape[0]
    return pl.pallas_call(
        _uv_kernel, out_shape=jax.ShapeDtypeStruct((T, 512), BF), grid=(1,),
        in_specs=[pl.BlockSpec((T, 2048), lambda i: (0, 0)),
                  pl.BlockSpec((MLA_HEADS // 2, 512, 128), lambda i: (0, 0, 0))],
        out_specs=pl.BlockSpec((T, 512), lambda i: (0, 0)),
        compiler_params=pltpu.CompilerParams(dimension_semantics=("arbitrary",), vmem_limit_bytes=VMEM_LIMIT),
        name="uv",
    )(ol, wuv)


NEW_PAD = 16
KEY_CH = 1024
KEY_GROUP = 8192
PATTN_SEQ = 2


def _pattn_kernel(pt_ref, ql_ref, qr_ref, cn_ref, kn_ref, cckv_hbm, ckr_hbm, o_ref, ckv_buf, kr_buf, kb_scr,
                  sem, *, n_pages, t_new):
    g = pl.program_id(0)
    ng = pl.num_programs(0)
    slot = g % 2
    n_keys = n_pages * PAGE

    def copies(step, sl):
        out = []
        for j in range(PATTN_SEQ):
            for pg in range(n_pages):
                page = pt_ref[step * PATTN_SEQ + j, pg]
                out.append(pltpu.make_async_copy(
                    cckv_hbm.at[page], ckv_buf.at[sl * PATTN_SEQ + j, pl.ds(pg * PAGE, PAGE), :], sem.at[0, sl]))
                out.append(pltpu.make_async_copy(
                    ckr_hbm.at[page], kr_buf.at[sl * PATTN_SEQ + j, :, pl.ds(pg * PAGE, PAGE)], sem.at[1, sl]))
        return out

    @pl.when(g == 0)
    def _():
        for c in copies(0, 0):
            c.start()

    for c in copies(jnp.minimum(g + 1, ng - 1), 1 - slot):
        c.start()
    for c in copies(g, slot):
        c.wait()

    for j in range(PATTN_SEQ):
        buf = slot * PATTN_SEQ + j
        ql = ql_ref[j]
        qr = qr_ref[j]
        R = ql.shape[0]
        cn = cn_ref[j].astype(BF)
        kn = kn_ref[j].astype(BF)
        s_new = _dot_nt(ql, cn) + _dot_nt(qr, kn)
        tok = lax.broadcasted_iota(jnp.int32, (R, NEW_PAD), 0) % t_new
        key = lax.broadcasted_iota(jnp.int32, (R, NEW_PAD), 1)
        s_new = jnp.where(key <= tok, s_new, NEG_INF)
        m_new = jnp.max(s_new, axis=1, keepdims=True)
        p_new = jnp.exp2(s_new - m_new)
        parts = [(m_new, jnp.sum(p_new, axis=1, keepdims=True), _dot(p_new.astype(BF), cn))]
        for grp in range(n_keys // KEY_GROUP):
            gs = slice(grp * KEY_GROUP, (grp + 1) * KEY_GROUP)
            for ch in range(grp * (KEY_GROUP // KEY_CH), (grp + 1) * (KEY_GROUP // KEY_CH)):
                ks = slice(ch * KEY_CH, (ch + 1) * KEY_CH)
                kb_scr[j, ks, :] = ckv_buf[buf, ks, :].astype(BF)
            s_g = _dot_nt(ql, kb_scr[j, gs, :]) + _dot(qr, kr_buf[buf, :, gs].astype(BF))
            m_g = jnp.max(s_g, axis=1, keepdims=True)
            p_g = jnp.exp2(s_g - m_g)
            parts.append((m_g, jnp.sum(p_g, axis=1, keepdims=True), _dot(p_g.astype(BF), kb_scr[j, gs, :])))
        m = functools.reduce(jnp.maximum, [pt[0] for pt in parts])
        l = sum(pt[1] * jnp.exp2(pt[0] - m) for pt in parts)
        acc = sum(pt[2] * jnp.exp2(pt[0] - m) for pt in parts)
        o_ref[j] = acc * (1.0 / l)

    @pl.when(g == ng - 1)
    def _():
        for c in copies(g, 1 - slot):
            c.wait()


def _pattn(page_table, ql, qr, cn, kn, cache_ckv, cache_krope_t, *, t_new):
    Bd, n_pages = page_table.shape
    R = ql.shape[1]
    n_keys = n_pages * PAGE
    G = PATTN_SEQ
    blk = lambda b, pt: (b, 0, 0)
    grid_spec = pltpu.PrefetchScalarGridSpec(
        num_scalar_prefetch=1, grid=(Bd // G,),
        in_specs=[pl.BlockSpec((G, R, MLA_KV_LORA), blk), pl.BlockSpec((G, R, MLA_ROPE), blk),
                  pl.BlockSpec((G, NEW_PAD, MLA_KV_LORA), blk), pl.BlockSpec((G, NEW_PAD, MLA_ROPE), blk),
                  pl.BlockSpec(memory_space=pl.ANY), pl.BlockSpec(memory_space=pl.ANY)],
        out_specs=pl.BlockSpec((G, R, MLA_KV_LORA), blk),
        scratch_shapes=[pltpu.VMEM((2 * G, n_keys, MLA_KV_LORA), F32), pltpu.VMEM((2 * G, MLA_ROPE, n_keys), F32),
                        pltpu.VMEM((G, n_keys, MLA_KV_LORA), BF), pltpu.SemaphoreType.DMA((2, 2))])
    return pl.pallas_call(
        functools.partial(_pattn_kernel, n_pages=n_pages, t_new=t_new),
        out_shape=jax.ShapeDtypeStruct((Bd, R, MLA_KV_LORA), F32), grid_spec=grid_spec,
        compiler_params=pltpu.CompilerParams(dimension_semantics=("arbitrary",), vmem_limit_bytes=VMEM_LIMIT),
        name="pattn",
    )(page_table, ql, qr, cn, kn, cache_ckv, cache_krope_t)


FFN_TM = 512
FF_CH = 256
N_FF_CH = D_FF // FF_CH


def _ffn_kernel(x_ref, og_ref, om_ref, lng_ref, lnb_ref, wo_ref, l1g_ref, l1b_ref, wf_ref,
                cw_ref, cb_ref, wd_ref, l2g_ref, l2b_ref, *rest, tm, inject):
    if inject:
        inj1_ref, inj2_ref, m1_ref, m2_ref, y_ref, a_ref, act_scr = rest
    else:
        halo0_ref, y_ref, alast_ref, act_scr, halo_scr = rest
        i = pl.program_id(1)

        @pl.when(i == 0)
        def _():
            halo_scr[...] = halo0_ref[...]

    h = _layer_norm(x_ref[...], lng_ref[...], lnb_ref[...])
    mix = _dot(jnp.concatenate([og_ref[...], om_ref[...]], axis=1), wo_ref[...])
    h1 = _layer_norm(DN_ALPHA * h + mix, l1g_ref[...], l1b_ref[...])
    h1b = h1.astype(BF)
    ridx = lax.broadcasted_iota(jnp.int32, (tm, FF_CH), 0)
    for c in range(N_FF_CH):
        cs = slice(c * FF_CH, (c + 1) * FF_CH)
        a = _dot(h1b, wf_ref[:, cs])
        up = _dot(h1b, wf_ref[:, D_FF + c * FF_CH:D_FF + (c + 1) * FF_CH])
        r1 = pltpu.roll(a, 1, 0)
        r2 = pltpu.roll(a, 2, 0)
        if inject:
            a1 = jnp.where(m1_ref[...] > 0.0, inj1_ref[:, cs], r1)
            a2 = jnp.where(m2_ref[...] > 0.0, inj2_ref[:, cs], r2)
            a_ref[:, cs] = a
        else:
            hl = halo_scr[:, cs]
            a1 = jnp.where(ridx == 0, hl[1:2, :], r1)
            a2 = jnp.where(ridx == 0, hl[0:1, :], jnp.where(ridx == 1, hl[1:2, :], r2))
            halo_scr[:, cs] = a[tm - 2:tm, :]
            alast_ref[0, :, cs] = a[tm - 2:tm, :]
        conv = cb_ref[:, cs] + cw_ref[0:1, cs] * a2 + cw_ref[1:2, cs] * a1 + cw_ref[2:3, cs] * a
        act_scr[:, cs] = (_silu(conv) * up).astype(BF)
    ffn = _dot(act_scr[...], wd_ref[...])
    y_ref[...] = _layer_norm(DN_ALPHA * h1 + ffn, l2g_ref[...], l2b_ref[...])


def _ffn(x, og, om, w, *, B, L, tm, halo0=None, inject=None):
    nt = L // tm
    T = B * L
    row = lambda b, i: (b * nt + i, 0)
    c2 = lambda b, i: (0, 0)
    once = pl.Buffered(1)
    in_specs = [pl.BlockSpec((tm, D_MODEL), row), pl.BlockSpec((tm, 512), row), pl.BlockSpec((tm, 512), row),
                pl.BlockSpec((1, D_MODEL), c2), pl.BlockSpec((1, D_MODEL), c2),
                pl.BlockSpec((D_MODEL, D_MODEL), c2, pipeline_mode=once),
                pl.BlockSpec((1, D_MODEL), c2), pl.BlockSpec((1, D_MODEL), c2),
                pl.BlockSpec((D_MODEL, 2 * D_FF), c2, pipeline_mode=once),
                pl.BlockSpec((3, D_FF), c2), pl.BlockSpec((1, D_FF), c2),
                pl.BlockSpec((D_FF, D_MODEL), c2, pipeline_mode=once),
                pl.BlockSpec((1, D_MODEL), c2), pl.BlockSpec((1, D_MODEL), c2)]
    args = [x, og, om, w["lng"], w["lnb"], w["wo"], w["l1g"], w["l1b"], w["wf"],
            w["cw"], w["cb"], w["wd"], w["l2g"], w["l2b"]]
    scratch = [pltpu.VMEM((tm, D_FF), BF)]
    if inject is not None:
        inj1, inj2, m1, m2 = inject
        in_specs += [pl.BlockSpec((tm, D_FF), row), pl.BlockSpec((tm, D_FF), row),
                     pl.BlockSpec((tm, 1), row), pl.BlockSpec((tm, 1), row)]
        args += [inj1, inj2, m1, m2]
        out_shape = [jax.ShapeDtypeStruct((T, D_MODEL), F32), jax.ShapeDtypeStruct((T, D_FF), F32)]
        out_specs = [pl.BlockSpec((tm, D_MODEL), row), pl.BlockSpec((tm, D_FF), row)]
    else:
        in_specs.append(pl.BlockSpec((2, D_FF), c2))
        args.append(halo0)
        out_shape = [jax.ShapeDtypeStruct((T, D_MODEL), F32), jax.ShapeDtypeStruct((B, 2, D_FF), F32)]
        out_specs = [pl.BlockSpec((tm, D_MODEL), row), pl.BlockSpec((1, 2, D_FF), lambda b, i: (b, 0, 0))]
        scratch.append(pltpu.VMEM((2, D_FF), F32))
    return pl.pallas_call(
        functools.partial(_ffn_kernel, tm=tm, inject=inject is not None),
        out_shape=out_shape, grid=(B, nt), in_specs=in_specs, out_specs=out_specs, scratch_shapes=scratch,
        compiler_params=pltpu.CompilerParams(dimension_semantics=("parallel", "arbitrary"),
                                             vmem_limit_bytes=VMEM_LIMIT),
        name="ffn_inject" if inject is not None else "ffn_seq",
    )(*args)


def _rope_tables(pos):
    half = MLA_ROPE // 2
    inv = ROPE_BASE ** (-jnp.arange(half, dtype=F32) / half)
    ang = pos.astype(F32)[:, None] * inv[None, :]
    cos = jnp.concatenate([jnp.cos(ang)] * 2, axis=1)
    sin = jnp.concatenate([jnp.sin(ang)] * 2, axis=1)
    pad = jnp.zeros((pos.shape[0], 128 - MLA_ROPE), F32)
    return (jnp.tile(cos, (1, MLA_HEADS)), jnp.tile(sin, (1, MLA_HEADS)),
            jnp.concatenate([cos, pad], axis=1), jnp.concatenate([sin, pad], axis=1))


def _swap_halves(w):
    half = w.shape[-1] // 2
    return jnp.concatenate([-w[..., half:], w[..., :half]], axis=-1)


def _prep_weights(ln_emb_g, ln_emb_b, w_in, w_gate_up, b_gate, gla_norm_g, mla_q_norm_g, mla_kv_norm_g,
                  w_uq, w_uk, w_uv, w_o, ln1_g, ln1_b, w_ffn_in, conv_w, conv_b, w_down, ln2_g, ln2_b):
    o_alr, o_cq, o_ckv, o_kr = 1536, 1552, 1936, 2192
    w_kr = w_in[:, o_kr:o_kr + MLA_ROPE]
    last = jnp.concatenate([w_kr, _swap_halves(w_kr), w_in[:, o_alr:o_cq],
                            jnp.zeros((D_MODEL, 128 - 2 * MLA_ROPE - GLA_RANK), F32)], axis=1)
    win = jnp.concatenate([w_in[:, o_cq:o_ckv], last, w_in[:, o_ckv:o_kr], w_in[:, :o_alr]],
                          axis=1).astype(BF)
    wg = jnp.zeros((128, 256), F32).at[LAST_ALR:LAST_ALR + GLA_RANK].set(w_gate_up).astype(BF)
    uq = w_uq.reshape(MLA_Q_LORA, MLA_HEADS, MLA_NOPE + MLA_ROPE)
    uq_r = uq[:, :, MLA_NOPE:]
    wuq = jnp.concatenate([uq[:, :, :MLA_NOPE].reshape(MLA_Q_LORA, 512), uq_r.reshape(MLA_Q_LORA, 256),
                           _swap_halves(uq_r).reshape(MLA_Q_LORA, 256)], axis=1).astype(BF)
    ukt = jnp.transpose(w_uk.reshape(MLA_KV_LORA, MLA_HEADS // 2, 2, MLA_NOPE), (1, 2, 3, 0))
    z64 = jnp.zeros_like(ukt[:, 0])
    wuk = jnp.concatenate([jnp.concatenate([ukt[:, 0], z64], axis=2),
                           jnp.concatenate([z64, ukt[:, 1]], axis=2)], axis=1).astype(BF)
    uv = jnp.transpose(w_uv.reshape(MLA_KV_LORA, MLA_HEADS, MLA_DV), (1, 0, 2))
    uv = uv.reshape(MLA_HEADS // 2, 2, MLA_KV_LORA, MLA_DV)
    zz = jnp.zeros_like(uv[:, 0])
    wuv = jnp.concatenate([jnp.concatenate([uv[:, 0], zz], axis=2), jnp.concatenate([zz, uv[:, 1]], axis=2)],
                          axis=1).astype(BF)
    r = lambda a: a.reshape(1, -1)
    return dict(lng=r(ln_emb_g), lnb=r(ln_emb_b), win=win, wg=wg, bg=r(b_gate), qng=r(mla_q_norm_g),
                kvg=r(mla_kv_norm_g), wuq=wuq, wuk=wuk, wuv=wuv, wo=w_o.astype(BF), l1g=r(ln1_g), l1b=r(ln1_b),
                wf=w_ffn_in.astype(BF), cw=conv_w, cb=r(conv_b),
                wd=w_down.astype(BF), l2g=r(ln2_g), l2b=r(ln2_b), gn=r(gla_norm_g))


def kernel(x_prompt, x_sample, cache_ckv, cache_krope, page_table, state_gla, state_conv, meta_tokens, ln_emb_g, ln_emb_b, w_in, w_gate_up, b_gate, gla_norm_g, mla_q_norm_g, mla_kv_norm_g, w_uq, w_uk, w_uv, w_o, ln1_g, ln1_b, w_ffn_in, conv_w, conv_b, w_down, ln2_g, ln2_b):
    B, S, _ = x_prompt.shape
    Bd, Td, _ = x_sample.shape
    past_len = page_table.shape[1] * PAGE
    n_small = N_META + Bd * Td
    w = _prep_weights(ln_emb_g, ln_emb_b, w_in, w_gate_up, b_gate, gla_norm_g, mla_q_norm_g, mla_kv_norm_g,
                      w_uq, w_uk, w_uv, w_o, ln1_g, ln1_b, w_ffn_in, conv_w, conv_b, w_down, ln2_g, ln2_b)

    xs = jnp.concatenate([meta_tokens.astype(F32), x_sample.reshape(Bd * Td, D_MODEL)], axis=0)
    pos_s = jnp.concatenate([jnp.arange(N_META, dtype=jnp.int32),
                             jnp.tile(past_len + jnp.arange(Td, dtype=jnp.int32), Bd)])
    (qg_s, kg_s, vg_s, la_s, gg_s, qlat_s, qr_s, ckv_s, kr_s, kcat_s) = _proj(
        xs, w, _rope_tables(pos_s), B=1, L=n_small, tm=n_small)

    M = N_META
    zero_state = jnp.zeros((1, GLA_HEADS, GLA_DK, GLA_DV), F32)
    og_m, s_meta = _gla(qg_s[:M], kg_s[:M], vg_s[:M], la_s[:M], gg_s[:M], zero_state, w["gn"],
                        B=1, L=M, C=M, tg=M)
    om_m = _attn(qlat_s[:M], qr_s[:M], kcat_s[:M], None, w["wuv"], B=1, L=M, tq=M)

    TP = 16
    padt = lambda a: jnp.pad(a[M:].reshape(Bd, Td, -1), ((0, 0), (0, TP - Td), (0, 0))).reshape(Bd * TP, -1)
    og_sp, gla_sample = _gla(padt(qg_s), padt(kg_s), padt(vg_s), padt(la_s), padt(gg_s), state_gla, w["gn"],
                             B=Bd // GLA_SEQ_PER_STEP, L=GLA_SEQ_PER_STEP * TP, C=TP, tg=GLA_SEQ_PER_STEP * TP,
                             independent=True)
    og_s = og_sp.reshape(Bd, TP, 512)[:, :Td].reshape(Bd * Td, 512)

    heads_first = lambda a, d: jnp.transpose(a[M:].reshape(Bd, Td, MLA_HEADS, d), (0, 2, 1, 3)).reshape(
        Bd, MLA_HEADS * Td, d)
    padn = lambda a: jnp.pad(a[M:].reshape(Bd, Td, -1), ((0, 0), (0, NEW_PAD - Td), (0, 0)))
    o_s = _pattn(page_table, heads_first(qlat_s, MLA_KV_LORA), heads_first(qr_s, MLA_ROPE),
                 padn(ckv_s), padn(kr_s), cache_ckv, jnp.swapaxes(cache_krope, 1, 2), t_new=Td)
    om_s = _uv(jnp.transpose(o_s.reshape(Bd, MLA_HEADS, Td, MLA_KV_LORA), (0, 2, 1, 3)).reshape(
        Bd * Td, MLA_HEADS * MLA_KV_LORA).astype(BF), w["wuv"])

    sc = state_conv.astype(F32)
    zrow = jnp.zeros((Bd, 1, D_FF), F32)
    inj1 = jnp.concatenate([sc[:, 1:2], zrow, zrow, zrow], axis=1).reshape(Bd * Td, D_FF)
    inj2 = jnp.concatenate([sc[:, 0:1], sc[:, 1:2], zrow, zrow], axis=1).reshape(Bd * Td, D_FF)
    zmeta = jnp.zeros((M, D_FF), F32)
    t_in = np.concatenate([np.arange(M), np.tile(np.arange(Td), Bd)])
    m1 = jnp.asarray((t_in == 0).astype(np.float32)[:, None])
    m2 = jnp.asarray((t_in <= 1).astype(np.float32)[:, None])
    y_s, a_s = _ffn(xs, jnp.concatenate([og_m, og_s], axis=0), jnp.concatenate([om_m, om_s], axis=0), w,
                    B=1, L=n_small, tm=176,
                    inject=(jnp.concatenate([zmeta, inj1], axis=0), jnp.concatenate([zmeta, inj2], axis=0), m1, m2))

    TM = 512
    xp = x_prompt.reshape(B * S, D_MODEL)
    pos_p = N_META + jnp.arange(S, dtype=jnp.int32)
    (qg_p, kg_p, vg_p, la_p, gg_p, qlat_p, qr_p, ckv_p, kr_p, kcat_p) = _proj(
        xp, w, _rope_tables(pos_p), B=B, L=S, tm=TM)
    og_p, gla_p = _gla(qg_p, kg_p, vg_p, la_p, gg_p, s_meta, w["gn"], B=B, L=S, C=GLA_CHUNK, tg=TM,
                       rows_per_step=GLA_ROWS_PER_STEP)
    om_p = _attn(qlat_p, qr_p, kcat_p, kcat_s[:M], w["wuv"], B=B, L=S, tq=ATTN_TQ, seq_per_step=ATTN_SEQ_PER_STEP)
    y_p, conv_prompt = _ffn(xp, og_p, om_p, w, B=B, L=S, tm=FFN_TM, halo0=a_s[M - 2:M])

    y_prompt = y_p.reshape(B, S, D_MODEL)
    y_sample = y_s[M:].reshape(Bd, Td, D_MODEL)
    ckv_prompt = jnp.concatenate([jnp.broadcast_to(ckv_s[None, :M], (B, M, MLA_KV_LORA)),
                                  ckv_p.reshape(B, S, MLA_KV_LORA)], axis=1)
    krope_prompt = jnp.concatenate([jnp.broadcast_to(kr_s[None, :M], (B, M, MLA_ROPE)),
                                    kr_p.reshape(B, S, MLA_ROPE)], axis=1)
    ckv_sample = ckv_s[M:].reshape(Bd, Td, MLA_KV_LORA)
    krope_sample = kr_s[M:].reshape(Bd, Td, MLA_ROPE)
    conv_sample = a_s[M:].reshape(Bd, Td, D_FF)[:, Td - 2:]
    return (y_prompt, y_sample, ckv_prompt, krope_prompt, gla_p, conv_prompt,
            ckv_sample, krope_sample, gla_sample, conv_sample)
```
